```python
import math
import jax, jax.numpy as jnp
from jax import lax
import numpy as np

D_MODEL = 2048
BATCH = 8
SEQ = 4096
DEPTH = 4

N_MIXERS = 3
N_GDN_LAYERS = (DEPTH + 2) // 3
N_FOX_LAYERS = (DEPTH + 1) // 3
N_LRU_LAYERS = DEPTH // 3

GDN_HEADS = 16
GDN_HEAD_DIM = D_MODEL // GDN_HEADS
GDN_WIDTH = GDN_HEADS * GDN_HEAD_DIM
GDN_CHUNK = 64
CONV_WIDTH = 4

FOX_HEADS = 16
FOX_HEAD_DIM = D_MODEL // FOX_HEADS
FOX_WIDTH = FOX_HEADS * FOX_HEAD_DIM
FOX_Q_BLOCK = 128

LRU_WIDTH = D_MODEL
LRU_BLOCKS = 16
LRU_BLOCK_DIM = LRU_WIDTH // LRU_BLOCKS
LRU_C = 8.0

N_GROUPS = 4
EXPERTS_PER_GROUP = 8
N_EXPERTS = N_GROUPS * EXPERTS_PER_GROUP
EXPERT_FF = D_MODEL // 4
TOP_K_INNER = 2

ALPHA = (2 * DEPTH) ** 0.25
BETA = (8 * DEPTH) ** -0.25
LN_EPS = 1e-5
NORM_EPS = 1e-6

kernel_name = "hybrid_gdn_fox_rglru_hmoe_deepnorm"


def layer_norm(x, gain, bias):
    xf = x.astype(jnp.float32)
    mu = jnp.mean(xf, axis=-1, keepdims=True)
    var = jnp.mean(jnp.square(xf - mu), axis=-1, keepdims=True)
    return ((xf - mu) * lax.rsqrt(var + LN_EPS) * gain + bias).astype(x.dtype)


def rms_norm(x, gain):
    xf = x.astype(jnp.float32)
    return xf * lax.rsqrt(jnp.mean(jnp.square(xf), axis=-1, keepdims=True) + NORM_EPS) * gain


def l2_normalize(x):
    return x * lax.rsqrt(jnp.sum(jnp.square(x), axis=-1, keepdims=True) + NORM_EPS)


def causal_depthwise_conv(x, w):
    width, channels = w.shape
    return lax.conv_general_dilated(
        x, w[:, None, :].astype(x.dtype), window_strides=(1,),
        padding=[(width - 1, 0)], dimension_numbers=("NWC", "WIO", "NWC"),
        feature_group_count=channels)


def chunk_gated_delta_rule(q, k, v, log_decay, beta):
    B, S, H, Dk = q.shape
    Dv = v.shape[-1]
    C = GDN_CHUNK
    N = S // C

    def chunks(t):
        t = t.reshape((B, N, C, H) + t.shape[3:])
        return jnp.moveaxis(t, 3, 1)

    q, k, v, beta = chunks(q), chunks(k), chunks(v), chunks(beta)
    g = jnp.cumsum(chunks(log_decay), axis=-1)
    causal = jnp.tril(jnp.ones((C, C), dtype=bool))
    strict = jnp.tril(jnp.ones((C, C), dtype=bool), k=-1)
    decay = jnp.exp(jnp.where(causal, g[..., :, None] - g[..., None, :], -jnp.inf))
    k_beta = k * beta[..., None]
    lower = jnp.where(strict, jnp.einsum('bhncd,bhnmd->bhncm', k_beta, k) * decay, 0.0)
    rhs = jnp.concatenate([v * beta[..., None], k_beta * jnp.exp(g)[..., None]], axis=-1)
    sol = lax.linalg.triangular_solve(lower + jnp.eye(C, dtype=lower.dtype), rhs,
                                      left_side=True, lower=True, unit_diagonal=True)
    u, k_cumdecay = sol[..., :Dv], sol[..., Dv:]
    attn = jnp.where(causal, jnp.einsum('bhncd,bhnmd->bhncm', q, k) * decay, 0.0)

    def step(state, inp):
        q_c, k_c, u_c, kcd_c, g_c, attn_c = inp
        v_new = u_c - jnp.einsum('bhcd,bhde->bhce', kcd_c, state)
        o_c = (jnp.einsum('bhcd,bhde->bhce', q_c * jnp.exp(g_c)[..., None], state)
               + jnp.einsum('bhcm,bhme->bhce', attn_c, v_new))
        g_last = g_c[..., -1:]
        state = (state * jnp.exp(g_last)[..., None]
                 + jnp.einsum('bhcd,bhce->bhde', k_c * jnp.exp(g_last - g_c)[..., None], v_new))
        return state, o_c

    xs = tuple(jnp.moveaxis(t, 2, 0) for t in (q, k, u, k_cumdecay, g, attn))
    state0 = jnp.zeros((B, H, Dk, Dv), jnp.float32)
    _, o = lax.scan(step, state0, xs)
    return jnp.transpose(o, (1, 0, 3, 2, 4)).reshape(B, S, H, Dv)


def gated_deltanet(x, w_in, conv_w, a_log, dt_bias, norm_w, w_out):
    B, S, _ = x.shape
    W, H, Dh = GDN_WIDTH, GDN_HEADS, GDN_HEAD_DIM
    proj = x @ w_in
    qkv = jax.nn.silu(causal_depthwise_conv(proj[..., :3 * W], conv_w)).astype(jnp.float32)
    z = proj[..., 3 * W:4 * W].astype(jnp.float32)
    b_logit = proj[..., 4 * W:4 * W + H].astype(jnp.float32)
    a_logit = proj[..., 4 * W + H:].astype(jnp.float32)
    q = l2_normalize(qkv[..., :W].reshape(B, S, H, Dh)) * Dh ** -0.5
    k = l2_normalize(qkv[..., W:2 * W].reshape(B, S, H, Dh))
    v = qkv[..., 2 * W:].reshape(B, S, H, Dh)
    beta = jax.nn.sigmoid(b_logit)
    log_decay = -jnp.exp(a_log.astype(jnp.float32)) * jax.nn.softplus(a_logit + dt_bias.astype(jnp.float32))
    o = chunk_gated_delta_rule(q, k, v, log_decay, beta)
    o = rms_norm(o, norm_w) * jax.nn.silu(z.reshape(B, S, H, Dh))
    return o.reshape(B, S, W).astype(x.dtype) @ w_out


def forgetting_attention(x, w_in, b_forget, q_norm_w, k_norm_w, w_out):
    B, S, _ = x.shape
    W, H, Dh, QB = FOX_WIDTH, FOX_HEADS, FOX_HEAD_DIM, FOX_Q_BLOCK
    proj = x @ w_in
    q = rms_norm(proj[..., :W].reshape(B, S, H, Dh), q_norm_w) * Dh ** -0.5
    k = rms_norm(proj[..., W:2 * W].reshape(B, S, H, Dh), k_norm_w)
    v = proj[..., 2 * W:3 * W].reshape(B, S, H, Dh).astype(jnp.float32)
    z = proj[..., 3 * W:4 * W].astype(jnp.float32)
    log_f = jax.nn.log_sigmoid(proj[..., 4 * W:].astype(jnp.float32) + b_forget)
    cum = jnp.transpose(jnp.cumsum(log_f, axis=1), (0, 2, 1))
    q, k, v = (jnp.transpose(t, (0, 2, 1, 3)) for t in (q, k, v))
    outs = []
    for blk in range(S // QB):
        start, end = blk * QB, (blk + 1) * QB
        s = (jnp.einsum('bhqd,bhkd->bhqk', q[:, :, start:end], k[:, :, :end])
             + cum[:, :, start:end, None] - cum[:, :, None, :end])
        mask = (start + jnp.arange(QB))[:, None] >= jnp.arange(end)[None, :]
        p = jax.nn.softmax(jnp.where(mask, s, -jnp.inf), axis=-1)
        outs.append(jnp.einsum('bhqk,bhkd->bhqd', p, v[:, :, :end]))
    o = jnp.transpose(jnp.concatenate(outs, axis=2), (0, 2, 1, 3)).reshape(B, S, W)
    o = o * jax.nn.sigmoid(z)
    return o.astype(x.dtype) @ w_out


def _linear_recurrence_combine(left, right):
    a1, b1 = left
    a2, b2 = right
    return a1 * a2, a2 * b1 + b2


def rglru_block(x, w_in, conv_w, conv_b, w_gate_a, b_gate_a, w_gate_x, b_gate_x, a_param, w_out):
    B, S, _ = x.shape
    W = LRU_WIDTH
    proj = x @ w_in
    y = jax.nn.gelu(proj[..., :W].astype(jnp.float32))
    u = (causal_depthwise_conv(proj[..., W:], conv_w) + conv_b).astype(jnp.float32)
    ub = u.reshape(B, S, LRU_BLOCKS, LRU_BLOCK_DIM)
    r = jax.nn.sigmoid(jnp.einsum('bsnd,nde->bsne', ub, w_gate_a.astype(jnp.float32)).reshape(B, S, W) + b_gate_a)
    i = jax.nn.sigmoid(jnp.einsum('bsnd,nde->bsne', ub, w_gate_x.astype(jnp.float32)).reshape(B, S, W) + b_gate_x)
    log_a = -LRU_C * r * jax.nn.softplus(-a_param.astype(jnp.float32))
    a = jnp.exp(log_a)
    b_in = jnp.sqrt(-jnp.expm1(2.0 * log_a)) * (i * u)
    _, h = lax.associative_scan(_linear_recurrence_combine, (a, b_in), axis=1)
    return (h * y).astype(x.dtype) @ w_out


def hierarchical_moe(x, w_router_group, w_router_expert, w_gate_up, w_down):
    B, S, D = x.shape
    t = x.reshape(-1, D)
    T = t.shape[0]
    group_probs = jax.nn.softmax((t @ w_router_group).astype(jnp.float32), axis=-1)
    g_prob, g_idx = lax.top_k(group_probs, 1)
    expert_logits = jnp.einsum('td,gde->tge', t, w_router_expert).astype(jnp.float32)
    sel_logits = jnp.take_along_axis(expert_logits, g_idx[:, :, None], axis=1)[:, 0]
    e_probs = jax.nn.softmax(sel_logits, axis=-1)
    top_p, top_i = lax.top_k(e_probs, TOP_K_INNER)
    top_p = top_p / jnp.sum(top_p, axis=-1, keepdims=True)
    weights = g_prob * top_p
    flat_idx = g_idx * EXPERTS_PER_GROUP + top_i
    dense_w = jnp.sum(jax.nn.one_hot(flat_idx, N_EXPERTS, dtype=jnp.float32) * weights[..., None],
                      axis=1).astype(x.dtype)
    out = jnp.zeros((T, D), x.dtype)
    for e in range(N_EXPERTS):
        gu = t @ w_gate_up[e]
        h = jax.nn.silu(gu[:, :EXPERT_FF]) * gu[:, EXPERT_FF:]
        out = out + dense_w[:, e:e + 1] * (h @ w_down[e])
    return out.reshape(B, S, D)


def setup_inputs(seed: int = 0) -> dict:
    key = jax.random.key(seed)
    ks = jax.random.split(key, 32)
    f32 = jnp.float32
    D, W = D_MODEL, GDN_WIDTH
    nrm = lambda k, shape, scale: jax.random.normal(k, shape, f32) * scale

    x = nrm(ks[0], (BATCH, SEQ, D), 1.0)
    ln_gain = 1.0 + nrm(ks[1], (DEPTH, 2, D), 0.02)
    ln_bias = nrm(ks[2], (DEPTH, 2, D), 0.02)

    gdn_w_in = nrm(ks[3], (N_GDN_LAYERS, D, 4 * W + 2 * GDN_HEADS), D ** -0.5)
    gdn_conv_w = nrm(ks[4], (N_GDN_LAYERS, CONV_WIDTH, 3 * W), CONV_WIDTH ** -0.5)
    gdn_a_log = jnp.log(jax.random.uniform(ks[5], (N_GDN_LAYERS, GDN_HEADS), f32, 1.0, 16.0))
    dt = jnp.exp(jax.random.uniform(ks[6], (N_GDN_LAYERS, GDN_HEADS), f32, math.log(1e-3), math.log(1e-1)))
    gdn_dt_bias = dt + jnp.log(-jnp.expm1(-dt))
    gdn_norm_w = 1.0 + nrm(ks[7], (N_GDN_LAYERS, GDN_HEAD_DIM), 0.02)
    gdn_w_out = nrm(ks[8], (N_GDN_LAYERS, W, D), W ** -0.5 * BETA)

    fox_w_in = nrm(ks[9], (N_FOX_LAYERS, D, 4 * FOX_WIDTH + FOX_HEADS), D ** -0.5)
    fox_b_forget = jax.random.uniform(ks[10], (N_FOX_LAYERS, FOX_HEADS), f32, 1.0, 4.0)
    fox_q_norm_w = 1.0 + nrm(ks[11], (N_FOX_LAYERS, FOX_HEAD_DIM), 0.02)
    fox_k_norm_w = 1.0 + nrm(ks[12], (N_FOX_LAYERS, FOX_HEAD_DIM), 0.02)
    fox_w_out = nrm(ks[13], (N_FOX_LAYERS, FOX_WIDTH, D), FOX_WIDTH ** -0.5 * BETA)

    lru_w_in = nrm(ks[14], (N_LRU_LAYERS, D, 2 * LRU_WIDTH), D ** -0.5)
    lru_conv_w = nrm(ks[15], (N_LRU_LAYERS, CONV_WIDTH, LRU_WIDTH), CONV_WIDTH ** -0.5)
    lru_conv_b = nrm(ks[16], (N_LRU_LAYERS, LRU_WIDTH), 0.02)
    lru_w_gate_a = nrm(ks[17], (N_LRU_LAYERS, LRU_BLOCKS, LRU_BLOCK_DIM, LRU_BLOCK_DIM), LRU_BLOCK_DIM ** -0.5)
    lru_b_gate_a = nrm(ks[18], (N_LRU_LAYERS, LRU_WIDTH), 0.02)
    lru_w_gate_x = nrm(ks[19], (N_LRU_LAYERS, LRU_BLOCKS, LRU_BLOCK_DIM, LRU_BLOCK_DIM), LRU_BLOCK_DIM ** -0.5)
    lru_b_gate_x = nrm(ks[20], (N_LRU_LAYERS, LRU_WIDTH), 0.02)
    a_pow_c = jax.random.uniform(ks[21], (N_LRU_LAYERS, LRU_WIDTH), f32, 0.9, 0.999)
    sig_l = a_pow_c ** (1.0 / LRU_C)
    lru_a_param = jnp.log(sig_l) - jnp.log1p(-sig_l)
    lru_w_out = nrm(ks[22], (N_LRU_LAYERS, LRU_WIDTH, D), LRU_WIDTH ** -0.5 * BETA)

    moe_w_router_group = nrm(ks[23], (DEPTH, D, N_GROUPS), D ** -0.5)
    moe_w_router_expert = nrm(ks[24], (DEPTH, N_GROUPS, D, EXPERTS_PER_GROUP), D ** -0.5)
    moe_w_gate_up = nrm(ks[25], (DEPTH, N_EXPERTS, D, 2 * EXPERT_FF), D ** -0.5)
    moe_w_down = nrm(ks[26], (DEPTH, N_EXPERTS, EXPERT_FF, D), EXPERT_FF ** -0.5 * BETA)

    return {
        "x": x, "ln_gain": ln_gain, "ln_bias": ln_bias,
        "gdn_w_in": gdn_w_in, "gdn_conv_w": gdn_conv_w, "gdn_a_log": gdn_a_log,
        "gdn_dt_bias": gdn_dt_bias, "gdn_norm_w": gdn_norm_w, "gdn_w_out": gdn_w_out,
        "fox_w_in": fox_w_in, "fox_b_forget": fox_b_forget, "fox_q_norm_w": fox_q_norm_w,
        "fox_k_norm_w": fox_k_norm_w, "fox_w_out": fox_w_out,
        "lru_w_in": lru_w_in, "lru_conv_w": lru_conv_w, "lru_conv_b": lru_conv_b,
        "lru_w_gate_a": lru_w_gate_a, "lru_b_gate_a": lru_b_gate_a,
        "lru_w_gate_x": lru_w_gate_x, "lru_b_gate_x": lru_b_gate_x,
        "lru_a_param": lru_a_param, "lru_w_out": lru_w_out,
        "moe_w_router_group": moe_w_router_group, "moe_w_router_expert": moe_w_router_expert,
        "moe_w_gate_up": moe_w_gate_up, "moe_w_down": moe_w_down,
    }


def reference(x, ln_gain, ln_bias,
              gdn_w_in, gdn_conv_w, gdn_a_log, gdn_dt_bias, gdn_norm_w, gdn_w_out,
              fox_w_in, fox_b_forget, fox_q_norm_w, fox_k_norm_w, fox_w_out,
              lru_w_in, lru_conv_w, lru_conv_b, lru_w_gate_a, lru_b_gate_a,
              lru_w_gate_x, lru_b_gate_x, lru_a_param, lru_w_out,
              moe_w_router_group, moe_w_router_expert, moe_w_gate_up, moe_w_down):
    h = x
    for layer in range(DEPTH):
        kind, j = layer % N_MIXERS, layer // N_MIXERS
        if kind == 0:
            mix = gated_deltanet(h, gdn_w_in[j], gdn_conv_w[j], gdn_a_log[j], gdn_dt_bias[j],
                                 gdn_norm_w[j], gdn_w_out[j])
        elif kind == 1:
            mix = forgetting_attention(h, fox_w_in[j], fox_b_forget[j], fox_q_norm_w[j],
                                       fox_k_norm_w[j], fox_w_out[j])
        else:
            mix = rglru_block(h, lru_w_in[j], lru_conv_w[j], lru_conv_b[j], lru_w_gate_a[j],
                              lru_b_gate_a[j], lru_w_gate_x[j], lru_b_gate_x[j], lru_a_param[j],
                              lru_w_out[j])
        h = layer_norm(ALPHA * h + mix, ln_gain[layer, 0], ln_bias[layer, 0])
        ffn = hierarchical_moe(h, moe_w_router_group[layer], moe_w_router_expert[layer],
                               moe_w_gate_up[layer], moe_w_down[layer])
        h = layer_norm(ALPHA * h + ffn, ln_gain[layer, 1], ln_bias[layer, 1])
    return h
```

```python
import functools
import math

import jax
import jax.numpy as jnp
from jax import lax
from jax.experimental import pallas as pl
from jax.experimental.pallas import tpu as pltpu

F32 = jnp.float32
BF16 = jnp.bfloat16
U32 = jnp.uint32
HIGHEST = lax.Precision.HIGHEST

D_MODEL = 2048
N_HEADS = 16
HEAD_DIM = 128
LANES = 128
SUBLANES = 8
CONV_WIDTH = 4
GDN_CHUNK = 64
LRU_C = 8.0
N_GROUPS = 4
EXPERTS_PER_GROUP = 8
N_EXPERTS = N_GROUPS * EXPERTS_PER_GROUP
EXPERT_FF = D_MODEL // 4
DEPTH = 4
ALPHA = (2 * DEPTH) ** 0.25
LN_EPS = 1e-5
NORM_EPS = 1e-6
NEG_BIG = -1e30

MOE_TILE = 256
VMEM_LIMIT = 56 * 1024 * 1024


def _cparams(*sem):
    return pltpu.CompilerParams(dimension_semantics=sem, vmem_limit_bytes=VMEM_LIMIT)


def _bdot(a, b):
    return jnp.dot(a.astype(BF16), b.astype(BF16), preferred_element_type=F32)


def _layer_norm(y, gain, bias):
    mu = jnp.mean(y, axis=-1, keepdims=True)
    yc = y - mu
    var = jnp.mean(yc * yc, axis=-1, keepdims=True)
    return yc * lax.rsqrt(var + LN_EPS) * gain + bias


def _proj_kernel(x_ref, w_ref, *rest, epilogue, head_major):
    o_ref = rest[-1]
    acc = jnp.dot(x_ref[...], w_ref[...], preferred_element_type=F32)
    nb = acc.shape[1] // LANES
    if epilogue == "gelu":
        acc = jax.nn.gelu(acc, approximate=True)
    for j in range(nb) if (head_major or epilogue == "rms") else ():
        blk = acc[:, j * LANES:(j + 1) * LANES]
        if epilogue == "rms":
            ms = jnp.mean(blk * blk, axis=-1, keepdims=True)
            blk = blk * lax.rsqrt(ms + NORM_EPS) * rest[0][:, j * LANES:(j + 1) * LANES]
        if head_major:
            o_ref[j] = blk.astype(o_ref.dtype)
        else:
            o_ref[:, j * LANES:(j + 1) * LANES] = blk.astype(o_ref.dtype)
    if not (head_major or epilogue == "rms"):
        o_ref[...] = acc.astype(o_ref.dtype)


def _proj(x, w, *, out_dtype, tm, tn, epilogue=None, row=None, head_major=False):
    T, K = x.shape
    N = w.shape[1]
    tm, tn = min(tm, T), min(tn, N)
    in_specs = [pl.BlockSpec((tm, K), lambda i, j: (i, 0)),
                pl.BlockSpec((K, tn), lambda i, j: (0, j))]
    args = [x, w]
    if row is not None:
        in_specs.append(pl.BlockSpec((1, tn), lambda i, j: (0, j)))
        args.append(row)
    if head_major:
        out_shape = jax.ShapeDtypeStruct((N // LANES, T, LANES), out_dtype)
        out_spec = pl.BlockSpec((tn // LANES, tm, LANES), lambda i, j: (j, i, 0))
    else:
        out_shape = jax.ShapeDtypeStruct((T, N), out_dtype)
        out_spec = pl.BlockSpec((tm, tn), lambda i, j: (i, j))
    return pl.pallas_call(
        functools.partial(_proj_kernel, epilogue=epilogue, head_major=head_major),
        grid=(T // tm, N // tn), in_specs=in_specs, out_specs=out_spec, out_shape=out_shape,
        compiler_params=_cparams("parallel", "parallel"))(*args)


def _outproj_ln_kernel(a_ref, w_ref, h_ref, g_ref, b_ref, o_ref):
    mix = jnp.dot(a_ref[...], w_ref[...], preferred_element_type=F32)
    o_ref[...] = _layer_norm(ALPHA * h_ref[...] + mix, g_ref[...], b_ref[...])


def _outproj_ln(a, w, h, gain, bias, *, tm=256):
    T, W = a.shape
    D = w.shape[1]
    tm = min(tm, T)
    return pl.pallas_call(
        _outproj_ln_kernel, grid=(T // tm,),
        in_specs=[pl.BlockSpec((tm, W), lambda i: (i, 0)),
                  pl.BlockSpec((W, D), lambda i: (0, 0)),
                  pl.BlockSpec((tm, D), lambda i: (i, 0)),
                  pl.BlockSpec((1, D), lambda i: (0, 0)),
                  pl.BlockSpec((1, D), lambda i: (0, 0))],
        out_specs=pl.BlockSpec((tm, D), lambda i: (i, 0)),
        out_shape=jax.ShapeDtypeStruct((T, D), F32),
        compiler_params=_cparams("parallel"))(a, w, h, gain, bias)


def _gdn_gates_kernel(x_ref, w_ref, alog_ref, dt_ref, o_ref):
    acc = jnp.dot(x_ref[...], w_ref[...], preferred_element_type=F32)
    lane = lax.broadcasted_iota(jnp.int32, acc.shape, 1)
    beta = jax.nn.sigmoid(acc)
    log_decay = -jnp.exp(alog_ref[...]) * jax.nn.softplus(acc + dt_ref[...])
    o_ref[...] = jnp.where(lane < N_HEADS, beta, log_decay)


def _gdn_gates(xb, w_ba, alog_row, dt_row, *, tm=1024):
    T, K = xb.shape
    tm = min(tm, T)
    return pl.pallas_call(
        _gdn_gates_kernel, grid=(T // tm,),
        in_specs=[pl.BlockSpec((tm, K), lambda i: (i, 0)),
                  pl.BlockSpec((K, LANES), lambda i: (0, 0)),
                  pl.BlockSpec((1, LANES), lambda i: (0, 0)),
                  pl.BlockSpec((1, LANES), lambda i: (0, 0))],
        out_specs=pl.BlockSpec((tm, LANES), lambda i: (i, 0)),
        out_shape=jax.ShapeDtypeStruct((T, LANES), F32),
        compiler_params=_cparams("parallel"))(xb, w_ba, alog_row, dt_row)


def _bmm(a, b):
    return jnp.einsum("hcm,hmd->hcd", a.astype(BF16), b.astype(BF16), preferred_element_type=F32)


def _bmm_nt(a, b):
    return jnp.einsum("hcd,hmd->hcm", a.astype(BF16), b.astype(BF16), preferred_element_type=F32)


def _gdn_chunk_kernel(q_ref, k_ref, v_ref, z_ref, gt_ref, cw_ref, nw_ref, o_ref, state, ext, *, C):
    H = N_HEADS
    c = pl.program_id(1)

    @pl.when(c == 0)
    def _():
        state[...] = jnp.zeros_like(state)
        ext[:, :, 0:SUBLANES, :] = jnp.zeros((3, H, SUBLANES, LANES), F32)

    acts = []
    for i, ref in enumerate((q_ref, k_ref, v_ref)):
        raw = ref[...].astype(F32)
        ext[i, :, SUBLANES:SUBLANES + C, :] = raw
        acc = None
        for kk in range(CONV_WIDTH):
            start = SUBLANES - (CONV_WIDTH - 1) + kk
            term = ext[i, :, pl.ds(start, C), :] * cw_ref[kk, i * H:(i + 1) * H]
            acc = term if acc is None else acc + term
        ext[i, :, 0:SUBLANES, :] = raw[:, C - SUBLANES:, :]
        acts.append(acc * jax.nn.sigmoid(acc))
    qc, kc, v = acts
    q = qc * lax.rsqrt(jnp.sum(qc * qc, axis=-1, keepdims=True) + NORM_EPS) * (HEAD_DIM ** -0.5)
    k = kc * lax.rsqrt(jnp.sum(kc * kc, axis=-1, keepdims=True) + NORM_EPS)

    gt = gt_ref[...]
    ri = lax.broadcasted_iota(jnp.int32, (C, C), 0)
    ci = lax.broadcasted_iota(jnp.int32, (C, C), 1)
    causal = ri >= ci
    strict = ri > ci
    g_all = jnp.dot(causal.astype(F32), gt, precision=HIGHEST, preferred_element_type=F32)
    sel = (lax.broadcasted_iota(jnp.int32, (H, LANES), 1)
           == lax.broadcasted_iota(jnp.int32, (H, LANES), 0) + H).astype(F32)
    g_t = lax.dot_general(sel, g_all, (((1,), (1,)), ((), ())), precision=HIGHEST,
                          preferred_element_type=F32)
    g_col = jnp.stack([g_all[:, H + h:H + h + 1] for h in range(H)])
    g_row = jnp.stack([g_t[h:h + 1, :] for h in range(H)])
    beta = jnp.stack([gt[:, h:h + 1] for h in range(H)])
    decay = jnp.exp(jnp.where(causal[None], g_col - g_row, -jnp.inf))
    exp_g = jnp.exp(g_col)
    g_last = g_col[:, C - 1:C, :]

    kb = k * beta
    lower = jnp.where(strict[None], _bmm_nt(kb, k) * decay, 0.0)
    eye = (ri == ci).astype(F32)[None]
    m = -lower
    tinv = eye + m
    for _ in range(int(math.log2(C)) - 1):
        m = _bmm(m, m)
        tinv = tinv + _bmm(tinv, m)
    rhs = jnp.concatenate([v * beta, kb * exp_g], axis=-1)
    sol = _bmm(tinv, rhs)
    u, w = sol[..., :HEAD_DIM], sol[..., HEAD_DIM:]
    attn = jnp.where(causal[None], _bmm_nt(q, k) * decay, 0.0)

    s = state[...]
    v_new = u - _bmm(w, s)
    o = _bmm(q * exp_g, s) + _bmm(attn, v_new)
    k_dec = k * jnp.exp(g_last - g_col)
    state[...] = s * jnp.exp(g_last) + jnp.einsum(
        "hcd,hce->hde", k_dec.astype(BF16), v_new.astype(BF16), preferred_element_type=F32)

    o = o * lax.rsqrt(jnp.mean(o * o, axis=-1, keepdims=True) + NORM_EPS) * nw_ref[...]
    z = z_ref[...].astype(F32)
    o = o * (z * jax.nn.sigmoid(z))
    for h in range(H):
        o_ref[:, h * LANES:(h + 1) * LANES] = o[h].astype(o_ref.dtype)


def _gdn_chunk(proj_hm, gates, conv_w, norm_w, *, B, S, C=GDN_CHUNK):
    H = N_HEADS
    T = B * S
    nc = S // C
    slab = lambda j: pl.BlockSpec((H, C, LANES), lambda b, c: (j, b * nc + c, 0))
    return pl.pallas_call(
        functools.partial(_gdn_chunk_kernel, C=C), grid=(B, nc),
        in_specs=[slab(0), slab(1), slab(2), slab(3),
                  pl.BlockSpec((C, LANES), lambda b, c: (b * nc + c, 0)),
                  pl.BlockSpec((CONV_WIDTH, 3 * H, 1, LANES), lambda b, c: (0, 0, 0, 0)),
                  pl.BlockSpec((1, 1, LANES), lambda b, c: (0, 0, 0))],
        out_specs=pl.BlockSpec((C, H * LANES), lambda b, c: (b * nc + c, 0)),
        out_shape=jax.ShapeDtypeStruct((T, H * LANES), BF16),
        scratch_shapes=[pltpu.VMEM((H, HEAD_DIM, HEAD_DIM), F32),
                        pltpu.VMEM((3, H, C + SUBLANES, LANES), F32)],
        compiler_params=_cparams("parallel", "arbitrary"))(
            proj_hm, proj_hm, proj_hm, proj_hm, gates, conv_w, norm_w)


def _gated_deltanet(xb, h, w_in, conv_w, a_log, dt_bias, norm_w, w_out, gain, bias, *, B, S):
    W, H = N_HEADS * HEAD_DIM, N_HEADS
    proj_hm = _proj(xb, w_in[:, :4 * W].astype(BF16), out_dtype=BF16, tm=1024, tn=512, head_major=True)
    w_ba = jnp.pad(w_in[:, 4 * W:], ((0, 0), (0, LANES - 2 * H))).astype(BF16)
    lane_pad = lambda v: jnp.pad(v.astype(F32), (H, LANES - 2 * H)).reshape(1, LANES)
    gates = _gdn_gates(xb, w_ba, lane_pad(a_log), lane_pad(dt_bias))
    o = _gdn_chunk(proj_hm, gates, conv_w.reshape(CONV_WIDTH, 3 * H, 1, LANES).astype(F32),
                   norm_w.reshape(1, 1, LANES).astype(F32), B=B, S=S)
    return _outproj_ln(o, w_out.astype(BF16), h, gain, bias)


def _fox_cum_kernel(x_ref, w_ref, b_ref, o_ref, carry, *, tiles_per_seq):
    i = pl.program_id(0)

    @pl.when(i % tiles_per_seq == 0)
    def _():
        carry[...] = jnp.zeros_like(carry)

    acc = jnp.dot(x_ref[...], w_ref[...], preferred_element_type=F32)
    log_f = jax.nn.log_sigmoid(acc + b_ref[...])
    tg = acc.shape[0]
    tri = (lax.broadcasted_iota(jnp.int32, (tg, tg), 0)
           >= lax.broadcasted_iota(jnp.int32, (tg, tg), 1)).astype(F32)
    cum = jnp.dot(tri, log_f, precision=HIGHEST, preferred_element_type=F32) + carry[...]
    carry[...] = cum[tg - 1:tg, :]
    sel = (lax.broadcasted_iota(jnp.int32, (N_HEADS, LANES), 1)
           == lax.broadcasted_iota(jnp.int32, (N_HEADS, LANES), 0)).astype(F32)
    o_ref[0] = lax.dot_general(sel, cum, (((1,), (1,)), ((), ())), precision=HIGHEST,
                               preferred_element_type=F32)


def _fox_cum(xb, w_f, b_row, *, B, S, tg=512):
    T, K = xb.shape
    tg = min(tg, S)
    nt = S // tg
    return pl.pallas_call(
        functools.partial(_fox_cum_kernel, tiles_per_seq=nt), grid=(T // tg,),
        in_specs=[pl.BlockSpec((tg, K), lambda i: (i, 0)),
                  pl.BlockSpec((K, LANES), lambda i: (0, 0)),
                  pl.BlockSpec((1, LANES), lambda i: (0, 0))],
        out_specs=pl.BlockSpec((1, N_HEADS, tg), lambda i: (i // nt, 0, i % nt)),
        out_shape=jax.ShapeDtypeStruct((B, N_HEADS, S), F32),
        scratch_shapes=[pltpu.VMEM((1, LANES), F32)],
        compiler_params=_cparams("arbitrary"))(xb, w_f, b_row)


def _fox_attn_kernel(q_ref, k_ref, v_ref, z_ref, cum_ref, o_ref, *, tq, tk):
    qi = pl.program_id(1)
    q = q_ref[0]
    ratio = tq // tk
    c0 = cum_ref[0, qi * ratio][:, 0:1]
    row = qi * tq + lax.broadcasted_iota(jnp.int32, (tq, tk), 0)
    col0 = lax.broadcasted_iota(jnp.int32, (tq, tk), 1)

    def body(kb, carry):
        m, l, acc = carry
        start = pl.multiple_of(kb * tk, tk)
        k = k_ref[0, pl.ds(start, tk), :]
        v = v_ref[0, pl.ds(start, tk), :]
        s = lax.dot_general(q, k, (((1,), (1,)), ((), ())), preferred_element_type=F32)
        s = s + (c0 - cum_ref[0, kb])
        s = jnp.where(row >= kb * tk + col0, s, NEG_BIG)
        m_new = jnp.maximum(m, jnp.max(s, axis=-1, keepdims=True))
        p = jnp.exp(s - m_new)
        scale = jnp.exp(m - m_new)
        l = scale * l + jnp.sum(p, axis=-1, keepdims=True)
        acc = scale * acc + jnp.dot(p.astype(BF16), v, preferred_element_type=F32)
        return m_new, l, acc

    init = (jnp.full((tq, 1), NEG_BIG, F32), jnp.zeros((tq, 1), F32), jnp.zeros((tq, HEAD_DIM), F32))
    _, l, acc = lax.fori_loop(0, (qi + 1) * ratio, body, init)
    z = z_ref[0].astype(F32)
    o_ref[...] = (acc / l * jax.nn.sigmoid(z)).astype(o_ref.dtype)


def _fox_attn(qk_hm, vz_hm, cum, *, B, S, tq=512, tk=512):
    H = N_HEADS
    T = B * S
    tq, tk = min(tq, S), min(tk, S)
    nq, nk = S // tq, S // tk
    cum4 = cum.reshape(B * H, nk, 1, tk)
    return pl.pallas_call(
        functools.partial(_fox_attn_kernel, tq=tq, tk=tk), grid=(B * H, nq),
        in_specs=[pl.BlockSpec((1, tq, LANES), lambda bh, qi: (bh % H, (bh // H) * nq + qi, 0)),
                  pl.BlockSpec((1, S, LANES), lambda bh, qi: (H + bh % H, bh // H, 0)),
                  pl.BlockSpec((1, S, LANES), lambda bh, qi: (bh % H, bh // H, 0)),
                  pl.BlockSpec((1, tq, LANES), lambda bh, qi: (H + bh % H, (bh // H) * nq + qi, 0)),
                  pl.BlockSpec((1, nk, 1, tk), lambda bh, qi: (bh, 0, 0, 0))],
        out_specs=pl.BlockSpec((tq, LANES), lambda bh, qi: ((bh // H) * nq + qi, bh % H)),
        out_shape=jax.ShapeDtypeStruct((T, H * LANES), BF16),
        compiler_params=_cparams("parallel", "parallel"))(qk_hm, qk_hm, vz_hm, vz_hm, cum4)


def _forgetting_attention(xb, h, w_in, b_forget, q_norm_w, k_norm_w, w_out, gain, bias, *, B, S):
    W, H = N_HEADS * HEAD_DIM, N_HEADS
    norm_row = jnp.concatenate([jnp.tile(q_norm_w.astype(F32) * HEAD_DIM ** -0.5, H),
                                jnp.tile(k_norm_w.astype(F32), H)]).reshape(1, 2 * W)
    qk_hm = _proj(xb, w_in[:, :2 * W].astype(BF16), out_dtype=BF16, tm=1024, tn=512,
                  epilogue="rms", row=norm_row, head_major=True)
    vz_hm = _proj(xb, w_in[:, 2 * W:4 * W].astype(BF16), out_dtype=BF16, tm=1024, tn=512, head_major=True)
    w_f = jnp.pad(w_in[:, 4 * W:], ((0, 0), (0, LANES - H))).astype(BF16)
    b_row = jnp.pad(b_forget.astype(F32), (0, LANES - H)).reshape(1, LANES)
    cum = _fox_cum(xb, w_f, b_row, B=B, S=S)
    o = _fox_attn(qk_hm, vz_hm, cum, B=B, S=S)
    return _outproj_ln(o, w_out.astype(BF16), h, gain, bias)


def _lru_kernel(u_ref, y_ref, cw_ref, cb_ref, wa_ref, ba_ref, wx_ref, bx_ref, ap_ref, o_ref,
                ext, a_s, b_s, h_s, hcar, *, ts):
    t = pl.program_id(1)
    W = u_ref.shape[1]

    @pl.when(t == 0)
    def _():
        ext[0:SUBLANES, :] = jnp.zeros((SUBLANES, W), F32)
        hcar[...] = jnp.zeros_like(hcar)

    raw = u_ref[...]
    ext[SUBLANES:SUBLANES + ts, :] = raw
    u = cb_ref[...]
    for kk in range(CONV_WIDTH):
        start = SUBLANES - (CONV_WIDTH - 1) + kk
        u = u + ext[pl.ds(start, ts), :] * cw_ref[kk:kk + 1, :]
    ext[0:SUBLANES, :] = raw[ts - SUBLANES:, :]

    ra, rx = [], []
    for n in range(W // LANES):
        ub = u[:, n * LANES:(n + 1) * LANES].astype(BF16)
        ra.append(jnp.dot(ub, wa_ref[n], preferred_element_type=F32))
        rx.append(jnp.dot(ub, wx_ref[n], preferred_element_type=F32))
    r = jax.nn.sigmoid(jnp.concatenate(ra, axis=-1) + ba_ref[...])
    gate_x = jax.nn.sigmoid(jnp.concatenate(rx, axis=-1) + bx_ref[...])
    log_a = -LRU_C * r * jax.nn.softplus(-ap_ref[...])
    a_s[...] = jnp.exp(log_a)
    b_s[...] = jnp.sqrt(1.0 - jnp.exp(2.0 * log_a)) * (gate_x * u)

    rowi = lax.broadcasted_iota(jnp.int32, (SUBLANES, W), 0)

    def sub(j, hprev):
        off = pl.multiple_of(j * SUBLANES, SUBLANES)
        aa = a_s[pl.ds(off, SUBLANES), :]
        bb = b_s[pl.ds(off, SUBLANES), :]
        for d in (1, 2, 4):
            keep = rowi >= d
            a_sh = pltpu.roll(aa, d, 0)
            b_sh = pltpu.roll(bb, d, 0)
            bb = jnp.where(keep, aa * b_sh + bb, bb)
            aa = jnp.where(keep, aa * a_sh, aa)
        hh = aa * hprev + bb
        h_s[pl.ds(off, SUBLANES), :] = hh
        return hh[SUBLANES - 1:SUBLANES, :]

    hcar[...] = lax.fori_loop(0, ts // SUBLANES, sub, hcar[...])
    o_ref[...] = (h_s[...] * y_ref[...].astype(F32)).astype(o_ref.dtype)


def _lru_scan(u_raw, y, conv_w, conv_b, w_a, b_a, w_x, b_x, a_param, *, B, S, ts=256):
    T, W = u_raw.shape
    ts = min(ts, S)
    nt = S // ts
    nb = W // LANES
    tok = lambda b, t: (b * nt + t, 0)
    fixed2 = lambda b, t: (0, 0)
    fixed3 = lambda b, t: (0, 0, 0)
    return pl.pallas_call(
        functools.partial(_lru_kernel, ts=ts), grid=(B, nt),
        in_specs=[pl.BlockSpec((ts, W), tok), pl.BlockSpec((ts, W), tok),
                  pl.BlockSpec((CONV_WIDTH, W), fixed2), pl.BlockSpec((1, W), fixed2),
                  pl.BlockSpec((nb, LANES, LANES), fixed3), pl.BlockSpec((1, W), fixed2),
                  pl.BlockSpec((nb, LANES, LANES), fixed3), pl.BlockSpec((1, W), fixed2),
                  pl.BlockSpec((1, W), fixed2)],
        out_specs=pl.BlockSpec((ts, W), tok),
        out_shape=jax.ShapeDtypeStruct((T, W), BF16),
        scratch_shapes=[pltpu.VMEM((ts + SUBLANES, W), F32), pltpu.VMEM((ts, W), F32),
                        pltpu.VMEM((ts, W), F32), pltpu.VMEM((ts, W), F32), pltpu.VMEM((1, W), F32)],
        compiler_params=_cparams("parallel", "arbitrary"))(
            u_raw, y, conv_w, conv_b, w_a, b_a, w_x, b_x, a_param)


def _rglru_block(xb, h, w_in, conv_w, conv_b, w_gate_a, b_gate_a, w_gate_x, b_gate_x, a_param, w_out,
                 gain, bias, *, B, S):
    W = D_MODEL
    row = lambda v: v.astype(F32).reshape(1, W)
    y = _proj(xb, w_in[:, :W].astype(BF16), out_dtype=BF16, tm=1024, tn=512, epilogue="gelu")
    u_raw = _proj(xb, w_in[:, W:].astype(BF16), out_dtype=F32, tm=1024, tn=512)
    o = _lru_scan(u_raw, y, conv_w.astype(F32), row(conv_b), w_gate_a.astype(BF16), row(b_gate_a),
                  w_gate_x.astype(BF16), row(b_gate_x), row(a_param), B=B, S=S)
    return _outproj_ln(o, w_out.astype(BF16), h, gain, bias)


def _router_kernel(h_ref, w_ref, meta_ref, cnt_ref, carry):
    i = pl.program_id(0)

    @pl.when(i == 0)
    def _():
        carry[...] = jnp.zeros_like(carry)

    logits = jnp.dot(h_ref[...], w_ref[...], precision=HIGHEST, preferred_element_type=F32)
    tm = logits.shape[0]
    lane = lax.broadcasted_iota(jnp.int32, (tm, LANES), 1).astype(F32)
    first = lambda mask: jnp.min(jnp.where(mask, lane, float(LANES)), axis=-1, keepdims=True)

    is_g = lane < N_GROUPS
    gmax = jnp.max(jnp.where(is_g, logits, -jnp.inf), axis=-1, keepdims=True)
    gsum = jnp.sum(jnp.where(is_g, jnp.exp(logits - gmax), 0.0), axis=-1, keepdims=True)
    g_prob = 1.0 / gsum
    g_idx = first(is_g & (logits == gmax))
    lo = N_GROUPS + EXPERTS_PER_GROUP * g_idx
    is_e = (lane >= lo) & (lane < lo + EXPERTS_PER_GROUP)
    l1 = jnp.max(jnp.where(is_e, logits, -jnp.inf), axis=-1, keepdims=True)
    i1 = first(is_e & (logits == l1))
    is_e2 = is_e & (lane != i1)
    l2 = jnp.max(jnp.where(is_e2, logits, -jnp.inf), axis=-1, keepdims=True)
    i2 = first(is_e2 & (logits == l2))
    e2 = jnp.exp(l2 - l1)
    w1 = g_prob / (1.0 + e2)
    w2 = g_prob * e2 / (1.0 + e2)
    ex1, ex2 = i1 - N_GROUPS, i2 - N_GROUPS

    hot1 = lane == ex1
    hot2 = lane == ex2
    hot = jnp.where(hot1 | hot2, 1.0, 0.0)
    tri = (lax.broadcasted_iota(jnp.int32, (tm, tm), 0)
           > lax.broadcasted_iota(jnp.int32, (tm, tm), 1))
    prefix = jnp.dot(tri.astype(BF16), hot.astype(BF16), preferred_element_type=F32) + carry[...]
    rank1 = jnp.sum(jnp.where(hot1, prefix, 0.0), axis=-1, keepdims=True)
    rank2 = jnp.sum(jnp.where(hot2, prefix, 0.0), axis=-1, keepdims=True)
    total = carry[...] + jnp.sum(hot, axis=0, keepdims=True)
    carry[...] = total
    cnt_ref[...] = jnp.broadcast_to(total[None], cnt_ref.shape)

    meta = jnp.zeros((tm, LANES), F32)
    for idx, val in enumerate((ex1, ex2, w1, w2, rank1, rank2)):
        meta = jnp.where(lane == idx, val, meta)
    meta_ref[...] = meta


def _router(h, w_router, *, tm=512):
    T, D = h.shape
    tm = min(tm, T)
    nt = T // tm
    return pl.pallas_call(
        _router_kernel, grid=(nt,),
        in_specs=[pl.BlockSpec((tm, D), lambda i: (i, 0)),
                  pl.BlockSpec((D, LANES), lambda i: (0, 0))],
        out_specs=[pl.BlockSpec((tm, LANES), lambda i: (i, 0)),
                   pl.BlockSpec((1, SUBLANES, LANES), lambda i: (i, 0, 0))],
        out_shape=[jax.ShapeDtypeStruct((T, LANES), F32),
                   jax.ShapeDtypeStruct((nt, SUBLANES, LANES), F32)],
        scratch_shapes=[pltpu.VMEM((1, LANES), F32)],
        compiler_params=_cparams("arbitrary"))(h, w_router)


def _pack_bf16_pairs(x):
    m = x.shape[1] // 2
    lo = pltpu.bitcast(x[:, :m].astype(BF16).astype(F32), U32)
    hi = pltpu.bitcast(x[:, m:].astype(BF16).astype(F32), U32)
    return (lo >> 16) | hi


def _unpack_bf16_pairs(p):
    return pltpu.bitcast(p << 16, F32), pltpu.bitcast(p & jnp.uint32(0xFFFF0000), F32)


SLAB_ROWS = D_MODEL // 2 // LANES


def _stage_stride(n):
    return n + SUBLANES


def _matrix_to_slabs(mat, stage, dst_ref, n):
    stride = _stage_stride(n)
    for j in range(SLAB_ROWS):
        stage[pl.ds(j * stride, n), :] = mat[:, j * LANES:(j + 1) * LANES]

    def body(r, _):
        dst_ref[r] = stage[pl.ds(r, SLAB_ROWS, stride=stride), :]
        return 0

    lax.fori_loop(0, n, body, 0)


def _slabs_to_matrix(src_ref, stage, n):
    stride = _stage_stride(n)

    def body(r, _):
        stage[pl.ds(r, SLAB_ROWS, stride=stride), :] = src_ref[r]
        return 0

    lax.fori_loop(0, n, body, 0)
    return jnp.concatenate([stage[pl.ds(j * stride, n), :] for j in range(SLAB_ROWS)], axis=-1)


def _stage_scratch(n):
    return pltpu.VMEM((SLAB_ROWS * _stage_stride(n), LANES), U32)


def _dispatch_kernel(pos_ref, pad_ref, h_ref, xs_ref, stage, slabs, zeros, sem, zsem, *, tile, n_tok):
    i = pl.program_id(0)
    base = i * tile

    def pad_copy(e):
        return pltpu.make_async_copy(zeros, xs_ref.at[pl.ds(pad_ref[e], MOE_TILE)], zsem)

    def tail_copy(t):
        return pltpu.make_async_copy(zeros, xs_ref.at[pl.ds(t * MOE_TILE, MOE_TILE)], zsem)

    @pl.when(i == 0)
    def _():
        zeros[...] = jnp.zeros_like(zeros)
        n_tiles = xs_ref.shape[0] // MOE_TILE
        for e in range(N_EXPERTS):
            @pl.when(pad_ref[e] >= 0)
            def _():
                pad_copy(e).start()
        lax.fori_loop(pad_ref[N_EXPERTS], n_tiles, lambda t, c: (tail_copy(t).start(), c)[1], 0)
        for e in range(N_EXPERTS):
            @pl.when(pad_ref[e] >= 0)
            def _():
                pad_copy(e).wait()
        lax.fori_loop(pad_ref[N_EXPERTS], n_tiles, lambda t, c: (tail_copy(t).wait(), c)[1], 0)

    _matrix_to_slabs(_pack_bf16_pairs(h_ref[...]), stage, slabs, tile)

    def issue(r, _):
        for slot in range(2):
            pltpu.make_async_copy(slabs.at[r], xs_ref.at[pos_ref[slot * n_tok + base + r]], sem).start()
        return 0

    lax.fori_loop(0, tile, issue, 0)
    for slot in range(2):
        pltpu.make_async_copy(slabs, xs_ref.at[pl.ds(0, tile)], sem).wait()


def _dispatch(pos, pad_start, h, *, n_rows, tile=256):
    T, D = h.shape
    tile = min(tile, T)
    return pl.pallas_call(
        functools.partial(_dispatch_kernel, tile=tile, n_tok=T),
        grid_spec=pltpu.PrefetchScalarGridSpec(
            num_scalar_prefetch=2, grid=(T // tile,),
            in_specs=[pl.BlockSpec((tile, D), lambda i, pos, pad: (i, 0))],
            out_specs=pl.BlockSpec(memory_space=pl.ANY),
            scratch_shapes=[_stage_scratch(tile), pltpu.VMEM((tile, SLAB_ROWS, LANES), U32),
                            pltpu.VMEM((MOE_TILE, SLAB_ROWS, LANES), U32),
                            pltpu.SemaphoreType.DMA, pltpu.SemaphoreType.DMA]),
        out_shape=jax.ShapeDtypeStruct((n_rows, SLAB_ROWS, LANES), U32),
        compiler_params=_cparams("arbitrary"))(pos, pad_start, h)


def _experts_kernel(te_ref, na_ref, xs_ref, wgu_ref, wd_ref, y_ref, stage):
    @pl.when(pl.program_id(0) >= na_ref[0])
    def _():
        y_ref[...] = jnp.zeros_like(y_ref)

    @pl.when(pl.program_id(0) < na_ref[0])
    def _():
        tm = xs_ref.shape[0]
        lo, hi = _unpack_bf16_pairs(_slabs_to_matrix(xs_ref, stage, tm))
        x = jnp.concatenate([lo, hi], axis=-1).astype(BF16)
        gu = jnp.dot(x, wgu_ref[0], preferred_element_type=F32)
        hid = jax.nn.silu(gu[:, :EXPERT_FF]) * gu[:, EXPERT_FF:]
        y = jnp.dot(hid.astype(BF16), wd_ref[0], preferred_element_type=F32)
        _matrix_to_slabs(_pack_bf16_pairs(y), stage, y_ref, tm)


def _experts(tile_expert, n_active, xs, w_gate_up, w_down):
    n_rows = xs.shape[0]
    D = D_MODEL
    tm = MOE_TILE
    row_map = lambda i, te, na: (jnp.minimum(i, na[0] - 1), 0, 0)
    w_map = lambda i, te, na: (te[jnp.minimum(i, na[0] - 1)], 0, 0)
    return pl.pallas_call(
        _experts_kernel,
        grid_spec=pltpu.PrefetchScalarGridSpec(
            num_scalar_prefetch=2, grid=(n_rows // tm,),
            in_specs=[pl.BlockSpec((tm, SLAB_ROWS, LANES), row_map),
                      pl.BlockSpec((1, D, 2 * EXPERT_FF), w_map),
                      pl.BlockSpec((1, EXPERT_FF, D), w_map)],
            out_specs=pl.BlockSpec((tm, SLAB_ROWS, LANES), lambda i, te, na: (i, 0, 0)),
            scratch_shapes=[_stage_scratch(tm)]),
        out_shape=jax.ShapeDtypeStruct((n_rows, SLAB_ROWS, LANES), U32),
        compiler_params=_cparams("arbitrary"))(tile_expert, n_active, xs, w_gate_up, w_down)


def _combine_ln_kernel(pos_ref, h_ref, meta_ref, g_ref, b_ref, ys_ref, o_ref, ob_ref,
                       slabs0, slabs1, stage, sem, *, tile, n_tok):
    base = pl.program_id(0) * tile
    slabs = (slabs0, slabs1)

    def issue(r, _):
        for slot in range(2):
            pltpu.make_async_copy(ys_ref.at[pos_ref[slot * n_tok + base + r]], slabs[slot].at[r], sem).start()
        return 0

    lax.fori_loop(0, tile, issue, 0)
    for slot in range(2):
        pltpu.make_async_copy(ys_ref.at[pl.ds(0, tile)], slabs[slot], sem).wait()

    meta = meta_ref[...]
    halves = [None, None]
    for slot in range(2):
        lo, hi = _unpack_bf16_pairs(_slabs_to_matrix(slabs[slot], stage, tile))
        wgt = meta[:, 2 + slot:3 + slot]
        halves = [wgt * part if acc is None else acc + wgt * part
                  for acc, part in zip(halves, (lo, hi))]
    ffn = jnp.concatenate(halves, axis=-1)
    out = _layer_norm(ALPHA * h_ref[...] + ffn, g_ref[...], b_ref[...])
    o_ref[...] = out
    ob_ref[...] = out.astype(BF16)


def _combine_ln(pos, h, meta, gain, bias, ys, *, tile=256):
    T, D = h.shape
    tile = min(tile, T)
    tok = lambda i, pos: (i, 0)
    fixed = lambda i, pos: (0, 0)
    slab_buf = pltpu.VMEM((tile, SLAB_ROWS, LANES), U32)
    return pl.pallas_call(
        functools.partial(_combine_ln_kernel, tile=tile, n_tok=T),
        grid_spec=pltpu.PrefetchScalarGridSpec(
            num_scalar_prefetch=1, grid=(T // tile,),
            in_specs=[pl.BlockSpec((tile, D), tok), pl.BlockSpec((tile, LANES), tok),
                      pl.BlockSpec((1, D), fixed), pl.BlockSpec((1, D), fixed),
                      pl.BlockSpec(memory_space=pl.ANY)],
            out_specs=[pl.BlockSpec((tile, D), tok), pl.BlockSpec((tile, D), tok)],
            scratch_shapes=[slab_buf, slab_buf, _stage_scratch(tile), pltpu.SemaphoreType.DMA]),
        out_shape=[jax.ShapeDtypeStruct((T, D), F32), jax.ShapeDtypeStruct((T, D), BF16)],
        compiler_params=_cparams("arbitrary"))(pos, h, meta, gain, bias, ys)


def _hierarchical_moe(h, w_router_group, w_router_expert, w_gate_up, w_down, gain, bias):
    T, D = h.shape
    w_router = jnp.concatenate(
        [w_router_group, jnp.transpose(w_router_expert, (1, 0, 2)).reshape(D, N_EXPERTS)], axis=1)
    w_router = jnp.pad(w_router.astype(F32), ((0, 0), (0, LANES - N_GROUPS - N_EXPERTS)))
    meta, counts = _router(h, w_router)

    cnt = counts[-1, 0, :N_EXPERTS].astype(jnp.int32)
    padded = (cnt + MOE_TILE - 1) // MOE_TILE * MOE_TILE
    ends = jnp.cumsum(padded)
    starts = ends - padded
    ex = meta[:, 0:2].astype(jnp.int32)
    rank = meta[:, 4:6].astype(jnp.int32)
    pos = (jnp.take(starts, ex) + rank).T.reshape(2 * T)
    n_rows = 2 * T + N_EXPERTS * MOE_TILE
    n_tiles = n_rows // MOE_TILE
    tile_expert = jnp.minimum(
        jnp.searchsorted(ends, jnp.arange(n_tiles, dtype=jnp.int32) * MOE_TILE, side="right"),
        N_EXPERTS - 1).astype(jnp.int32)
    n_active = (ends[-1:] // MOE_TILE).astype(jnp.int32)
    pad_start = jnp.concatenate([jnp.where(padded > 0, ends - MOE_TILE, -1).astype(jnp.int32), n_active])

    xs = _dispatch(pos, pad_start, h, n_rows=n_rows)
    ys = _experts(tile_expert, n_active, xs, w_gate_up.astype(BF16), w_down.astype(BF16))
    return _combine_ln(pos, h, meta, gain, bias, ys)


def kernel(x, ln_gain, ln_bias, gdn_w_in, gdn_conv_w, gdn_a_log, gdn_dt_bias, gdn_norm_w, gdn_w_out, fox_w_in, fox_b_forget, fox_q_norm_w, fox_k_norm_w, fox_w_out, lru_w_in, lru_conv_w, lru_conv_b, lru_w_gate_a, lru_b_gate_a, lru_w_gate_x, lru_b_gate_x, lru_a_param, lru_w_out, moe_w_router_group, moe_w_router_expert, moe_w_gate_up, moe_w_down):
    B, S, D = x.shape
    h = x.reshape(B * S, D).astype(F32)
    hb = h.astype(BF16)
    row = lambda v: v.astype(F32).reshape(1, D)
    for layer in range(DEPTH):
        kind, j = layer % 3, layer // 3
        g0, b0 = row(ln_gain[layer, 0]), row(ln_bias[layer, 0])
        g1, b1 = row(ln_gain[layer, 1]), row(ln_bias[layer, 1])
        if kind == 0:
            h = _gated_deltanet(hb, h, gdn_w_in[j], gdn_conv_w[j], gdn_a_log[j], gdn_dt_bias[j],
                                gdn_norm_w[j], gdn_w_out[j], g0, b0, B=B, S=S)
        elif kind == 1:
            h = _forgetting_attention(hb, h, fox_w_in[j], fox_b_forget[j], fox_q_norm_w[j],
                                      fox_k_norm_w[j], fox_w_out[j], g0, b0, B=B, S=S)
        else:
            h = _rglru_block(hb, h, lru_w_in[j], lru_conv_w[j], lru_conv_b[j], lru_w_gate_a[j],
                             lru_b_gate_a[j], lru_w_gate_x[j], lru_b_gate_x[j], lru_a_param[j],
                             lru_w_out[j], g0, b0, B=B, S=S)
        h, hb = _hierarchical_moe(h, moe_w_router_group[layer], moe_w_router_expert[layer],
                                  moe_w_gate_up[layer], moe_w_down[layer], g1, b1)
    return h.reshape(B, S, D).astype(x.dtype)
```

```python
import functools
import math

import jax
import jax.numpy as jnp
from jax import lax
from jax.experimental import pallas as pl
from jax.experimental.pallas import tpu as pltpu

F32 = jnp.float32
BF16 = jnp.bfloat16
U32 = jnp.uint32
HIGHEST = lax.Precision.HIGHEST

D_MODEL = 2048
N_HEADS = 16
HEAD_DIM = 128
LANES = 128
SUBLANES = 8
CONV_WIDTH = 4
GDN_CHUNK = 64
LRU_C = 8.0
N_GROUPS = 4
EXPERTS_PER_GROUP = 8
N_EXPERTS = N_GROUPS * EXPERTS_PER_GROUP
EXPERT_FF = D_MODEL // 4
DEPTH = 4
ALPHA = (2 * DEPTH) ** 0.25
LN_EPS = 1e-5
NORM_EPS = 1e-6
NEG_BIG = -1e30
LOG2E = math.log2(math.e)

MOE_TILE = 256
VMEM_LIMIT = 56 * 1024 * 1024


def _cparams(*sem):
    return pltpu.CompilerParams(dimension_semantics=sem, vmem_limit_bytes=VMEM_LIMIT)


def _bdot(a, b):
    return jnp.dot(a.astype(BF16), b.astype(BF16), preferred_element_type=F32)


def _layer_norm(y, gain, bias):
    mu = jnp.mean(y, axis=-1, keepdims=True)
    yc = y - mu
    var = jnp.mean(yc * yc, axis=-1, keepdims=True)
    return yc * lax.rsqrt(var + LN_EPS) * gain + bias


def _proj_kernel(x_ref, w_ref, *rest, epilogue, head_major):
    o_ref = rest[-1]
    acc = jnp.dot(x_ref[...], w_ref[...], preferred_element_type=F32)
    nb = acc.shape[1] // LANES
    if epilogue == "gelu":
        acc = jax.nn.gelu(acc, approximate=True)
    for j in range(nb) if (head_major or epilogue == "rms") else ():
        blk = acc[:, j * LANES:(j + 1) * LANES]
        if epilogue == "rms":
            ms = jnp.mean(blk * blk, axis=-1, keepdims=True)
            blk = blk * lax.rsqrt(ms + NORM_EPS) * rest[0][:, j * LANES:(j + 1) * LANES]
        if head_major:
            o_ref[j] = blk.astype(o_ref.dtype)
        else:
            o_ref[:, j * LANES:(j + 1) * LANES] = blk.astype(o_ref.dtype)
    if not (head_major or epilogue == "rms"):
        o_ref[...] = acc.astype(o_ref.dtype)


def _proj(x, w, *, out_dtype, tm, tn, epilogue=None, row=None, head_major=False):
    T, K = x.shape
    N = w.shape[1]
    tm, tn = min(tm, T), min(tn, N)
    in_specs = [pl.BlockSpec((tm, K), lambda i, j: (i, 0)),
                pl.BlockSpec((K, tn), lambda i, j: (0, j))]
    args = [x, w]
    if row is not None:
        in_specs.append(pl.BlockSpec((1, tn), lambda i, j: (0, j)))
        args.append(row)
    if head_major:
        out_shape = jax.ShapeDtypeStruct((N // LANES, T, LANES), out_dtype)
        out_spec = pl.BlockSpec((tn // LANES, tm, LANES), lambda i, j: (j, i, 0))
    else:
        out_shape = jax.ShapeDtypeStruct((T, N), out_dtype)
        out_spec = pl.BlockSpec((tm, tn), lambda i, j: (i, j))
    return pl.pallas_call(
        functools.partial(_proj_kernel, epilogue=epilogue, head_major=head_major),
        grid=(T // tm, N // tn), in_specs=in_specs, out_specs=out_spec, out_shape=out_shape,
        name="proj" + ("_" + epilogue if epilogue else ""),
        compiler_params=_cparams("parallel", "parallel"))(*args)


def _outproj_ln_kernel(a_ref, w_ref, h_ref, g_ref, b_ref, o_ref):
    mix = jnp.dot(a_ref[...], w_ref[...], preferred_element_type=F32)
    o_ref[...] = _layer_norm(ALPHA * h_ref[...] + mix, g_ref[...], b_ref[...])


def _outproj_ln(a, w, h, gain, bias, *, tm=256):
    T, W = a.shape
    D = w.shape[1]
    tm = min(tm, T)
    return pl.pallas_call(
        _outproj_ln_kernel, grid=(T // tm,),
        in_specs=[pl.BlockSpec((tm, W), lambda i: (i, 0)),
                  pl.BlockSpec((W, D), lambda i: (0, 0)),
                  pl.BlockSpec((tm, D), lambda i: (i, 0)),
                  pl.BlockSpec((1, D), lambda i: (0, 0)),
                  pl.BlockSpec((1, D), lambda i: (0, 0))],
        out_specs=pl.BlockSpec((tm, D), lambda i: (i, 0)),
        out_shape=jax.ShapeDtypeStruct((T, D), F32),
        name="outproj_ln", compiler_params=_cparams("parallel"))(a, w, h, gain, bias)


def _gdn_gates_kernel(x_ref, w_ref, alog_ref, dt_ref, o_ref):
    acc = jnp.dot(x_ref[...], w_ref[...], preferred_element_type=F32)
    lane = lax.broadcasted_iota(jnp.int32, acc.shape, 1)
    beta = jax.nn.sigmoid(acc)
    log_decay = -jnp.exp(alog_ref[...]) * jax.nn.softplus(acc + dt_ref[...])
    o_ref[...] = jnp.where(lane < N_HEADS, beta, log_decay)


def _gdn_gates(xb, w_ba, alog_row, dt_row, *, tm=1024):
    T, K = xb.shape
    tm = min(tm, T)
    return pl.pallas_call(
        _gdn_gates_kernel, grid=(T // tm,),
        in_specs=[pl.BlockSpec((tm, K), lambda i: (i, 0)),
                  pl.BlockSpec((K, LANES), lambda i: (0, 0)),
                  pl.BlockSpec((1, LANES), lambda i: (0, 0)),
                  pl.BlockSpec((1, LANES), lambda i: (0, 0))],
        out_specs=pl.BlockSpec((tm, LANES), lambda i: (i, 0)),
        out_shape=jax.ShapeDtypeStruct((T, LANES), F32),
        name="gdn_gates", compiler_params=_cparams("parallel"))(xb, w_ba, alog_row, dt_row)


def _bmm(a, b):
    return jnp.einsum("hcm,hmd->hcd", a.astype(BF16), b.astype(BF16), preferred_element_type=F32)


def _bmm_nt(a, b):
    return jnp.einsum("hcd,hmd->hcm", a.astype(BF16), b.astype(BF16), preferred_element_type=F32)


def _gdn_chunk_kernel(q_ref, k_ref, v_ref, z_ref, gt_ref, cw_ref, nw_ref, o_ref, state, ext, *, C):
    H = N_HEADS
    c = pl.program_id(1)

    @pl.when(c == 0)
    def _():
        state[...] = jnp.zeros_like(state)
        ext[:, :, 0:SUBLANES, :] = jnp.zeros((3, H, SUBLANES, LANES), F32)

    acts = []
    for i, ref in enumerate((q_ref, k_ref, v_ref)):
        raw = ref[...].astype(F32)
        ext[i, :, SUBLANES:SUBLANES + C, :] = raw
        acc = None
        for kk in range(CONV_WIDTH):
            start = SUBLANES - (CONV_WIDTH - 1) + kk
            term = ext[i, :, pl.ds(start, C), :] * cw_ref[kk, i * H:(i + 1) * H]
            acc = term if acc is None else acc + term
        ext[i, :, 0:SUBLANES, :] = raw[:, C - SUBLANES:, :]
        acts.append(acc * jax.nn.sigmoid(acc))
    qc, kc, v = acts
    q = qc * lax.rsqrt(jnp.sum(qc * qc, axis=-1, keepdims=True) + NORM_EPS) * (HEAD_DIM ** -0.5)
    k = kc * lax.rsqrt(jnp.sum(kc * kc, axis=-1, keepdims=True) + NORM_EPS)

    gt = gt_ref[...]
    ri = lax.broadcasted_iota(jnp.int32, (C, C), 0)
    ci = lax.broadcasted_iota(jnp.int32, (C, C), 1)
    causal = ri >= ci
    strict = ri > ci
    g_all = jnp.dot(causal.astype(F32), gt, precision=HIGHEST, preferred_element_type=F32)
    sel = (lax.broadcasted_iota(jnp.int32, (H, LANES), 1)
           == lax.broadcasted_iota(jnp.int32, (H, LANES), 0) + H).astype(F32)
    g_t = lax.dot_general(sel, g_all, (((1,), (1,)), ((), ())), precision=HIGHEST,
                          preferred_element_type=F32)
    g_col = jnp.stack([g_all[:, H + h:H + h + 1] for h in range(H)])
    g_row = jnp.stack([g_t[h:h + 1, :] for h in range(H)])
    beta = jnp.stack([gt[:, h:h + 1] for h in range(H)])
    decay = jnp.exp(jnp.where(causal[None], g_col - g_row, -jnp.inf))
    exp_g = jnp.exp(g_col)
    g_last = g_col[:, C - 1:C, :]

    kb = k * beta
    lower = jnp.where(strict[None], _bmm_nt(kb, k) * decay, 0.0)
    eye = (ri == ci).astype(F32)[None]
    m = -lower
    tinv = eye + m
    for _ in range(int(math.log2(C)) - 1):
        m = _bmm(m, m)
        tinv = tinv + _bmm(tinv, m)
    rhs = jnp.concatenate([v * beta, kb * exp_g], axis=-1)
    sol = _bmm(tinv, rhs)
    u, w = sol[..., :HEAD_DIM], sol[..., HEAD_DIM:]
    attn = jnp.where(causal[None], _bmm_nt(q, k) * decay, 0.0)

    s = state[...]
    v_new = u - _bmm(w, s)
    o = _bmm(q * exp_g, s) + _bmm(attn, v_new)
    k_dec = k * jnp.exp(g_last - g_col)
    state[...] = s * jnp.exp(g_last) + jnp.einsum(
        "hcd,hce->hde", k_dec.astype(BF16), v_new.astype(BF16), preferred_element_type=F32)

    o = o * lax.rsqrt(jnp.mean(o * o, axis=-1, keepdims=True) + NORM_EPS) * nw_ref[...]
    z = z_ref[...].astype(F32)
    o = o * (z * jax.nn.sigmoid(z))
    for h in range(H):
        o_ref[:, h * LANES:(h + 1) * LANES] = o[h].astype(o_ref.dtype)


def _gdn_chunk(proj_hm, gates, conv_w, norm_w, *, B, S, C=GDN_CHUNK):
    H = N_HEADS
    T = B * S
    nc = S // C
    slab = lambda j: pl.BlockSpec((H, C, LANES), lambda b, c: (j, b * nc + c, 0))
    return pl.pallas_call(
        functools.partial(_gdn_chunk_kernel, C=C), grid=(B, nc),
        in_specs=[slab(0), slab(1), slab(2), slab(3),
                  pl.BlockSpec((C, LANES), lambda b, c: (b * nc + c, 0)),
                  pl.BlockSpec((CONV_WIDTH, 3 * H, 1, LANES), lambda b, c: (0, 0, 0, 0)),
                  pl.BlockSpec((1, 1, LANES), lambda b, c: (0, 0, 0))],
        out_specs=pl.BlockSpec((C, H * LANES), lambda b, c: (b * nc + c, 0)),
        out_shape=jax.ShapeDtypeStruct((T, H * LANES), BF16),
        scratch_shapes=[pltpu.VMEM((H, HEAD_DIM, HEAD_DIM), F32),
                        pltpu.VMEM((3, H, C + SUBLANES, LANES), F32)],
        name="gdn_chunk", compiler_params=_cparams("parallel", "arbitrary"))(
            proj_hm, proj_hm, proj_hm, proj_hm, gates, conv_w, norm_w)


def _gated_deltanet(xb, h, w_in, conv_w, a_log, dt_bias, norm_w, w_out, gain, bias, *, B, S):
    W, H = N_HEADS * HEAD_DIM, N_HEADS
    proj_hm = _proj(xb, w_in[:, :4 * W].astype(BF16), out_dtype=BF16, tm=1024, tn=512, head_major=True)
    w_ba = jnp.pad(w_in[:, 4 * W:], ((0, 0), (0, LANES - 2 * H))).astype(BF16)
    lane_pad = lambda v: jnp.pad(v.astype(F32), (H, LANES - 2 * H)).reshape(1, LANES)
    gates = _gdn_gates(xb, w_ba, lane_pad(a_log), lane_pad(dt_bias))
    o = _gdn_chunk(proj_hm, gates, conv_w.reshape(CONV_WIDTH, 3 * H, 1, LANES).astype(F32),
                   norm_w.reshape(1, 1, LANES).astype(F32), B=B, S=S)
    return _outproj_ln(o, w_out.astype(BF16), h, gain, bias)


def _fox_cum_kernel(x_ref, w_ref, b_ref, o_ref, carry, *, tiles_per_seq):
    i = pl.program_id(0)

    @pl.when(i % tiles_per_seq == 0)
    def _():
        carry[...] = jnp.zeros_like(carry)

    acc = jnp.dot(x_ref[...], w_ref[...], preferred_element_type=F32)
    log_f = jax.nn.log_sigmoid(acc + b_ref[...])
    tg = acc.shape[0]
    tri = (lax.broadcasted_iota(jnp.int32, (tg, tg), 0)
           >= lax.broadcasted_iota(jnp.int32, (tg, tg), 1)).astype(F32)
    cum = jnp.dot(tri, log_f, precision=HIGHEST, preferred_element_type=F32) + carry[...]
    carry[...] = cum[tg - 1:tg, :]
    o_ref[...] = cum


def _fox_cum(xb, w_f, b_row, *, S, tg=512):
    T, K = xb.shape
    tg = min(tg, S)
    return pl.pallas_call(
        functools.partial(_fox_cum_kernel, tiles_per_seq=S // tg), grid=(T // tg,),
        in_specs=[pl.BlockSpec((tg, K), lambda i: (i, 0)),
                  pl.BlockSpec((K, LANES), lambda i: (0, 0)),
                  pl.BlockSpec((1, LANES), lambda i: (0, 0))],
        out_specs=pl.BlockSpec((tg, LANES), lambda i: (i, 0)),
        out_shape=jax.ShapeDtypeStruct((T, LANES), F32),
        scratch_shapes=[pltpu.VMEM((1, LANES), F32)],
        name="fox_cum", compiler_params=_cparams("arbitrary"))(xb, w_f, b_row)


def _split3_bf16(x):
    hi = x.astype(BF16).astype(F32)
    mid = (x - hi).astype(BF16).astype(F32)
    lo = (x - hi - mid).astype(BF16).astype(F32)
    return hi, mid, lo


FOX_HEADS_PER_STEP = 4


def _fox_attn_kernel(q_ref, k_ref, v_ref, z_ref, cum_ref, o_ref, k_aug, v_aug, *, blk):
    HP = FOX_HEADS_PER_STEP
    group = pl.program_id(0) % (N_HEADS // HP)
    qi = pl.program_id(1)
    S = k_ref.shape[1]

    @pl.when(qi == 0)
    def _():
        lane_row = lax.broadcasted_iota(jnp.int32, (1, LANES), 1)
        lane = lax.broadcasted_iota(jnp.int32, (S, LANES), 1)
        for j in range(HP):
            pick = (lane_row == group * HP + j).astype(F32)
            bias = -LOG2E * jnp.sum(cum_ref[...] * pick, axis=-1, keepdims=True)
            hi, mid, lo = _split3_bf16(bias)
            k_aug[j, :, :HEAD_DIM] = k_ref[j]
            k_aug[j, :, HEAD_DIM:] = jnp.where(
                lane == 0, hi, jnp.where(lane == 1, mid, jnp.where(lane == 2, lo, 0.0))).astype(BF16)
            v_aug[j, :, :HEAD_DIM] = v_ref[j]
            v_aug[j, :, HEAD_DIM:] = jnp.where(lane == 0, 1.0, 0.0).astype(BF16)

    lane_q = lax.broadcasted_iota(jnp.int32, (blk, LANES), 1)
    ones3 = jnp.where(lane_q < 3, 1.0, 0.0).astype(BF16)
    q = [jnp.concatenate([q_ref[j], ones3], axis=-1) for j in range(HP)]

    def step(kb, carry, masked):
        start = pl.multiple_of(kb * blk, blk)
        out = []
        for j in range(HP):
            m, acc = carry[j]
            s = lax.dot_general(q[j], k_aug[j, pl.ds(start, blk), :], (((1,), (1,)), ((), ())),
                                preferred_element_type=F32)
            if masked:
                keep = (lax.broadcasted_iota(jnp.int32, (blk, blk), 0)
                        >= lax.broadcasted_iota(jnp.int32, (blk, blk), 1))
                s = jnp.where(keep, s, NEG_BIG)
            m_new = jnp.maximum(m, jnp.max(s, axis=-1, keepdims=True))
            p = jnp.exp2(s - m_new)
            acc = jnp.exp2(m - m_new) * acc + jnp.dot(p.astype(BF16), v_aug[j, pl.ds(start, blk), :],
                                                      preferred_element_type=F32)
            out.append((m_new, acc))
        return tuple(out)

    init = tuple((jnp.full((blk, 1), NEG_BIG, F32), jnp.zeros((blk, 2 * HEAD_DIM), F32)) for _ in range(HP))
    carry = lax.fori_loop(0, qi, functools.partial(step, masked=False), init)
    carry = step(qi, carry, masked=True)
    for j in range(HP):
        acc = carry[j][1]
        z = z_ref[j].astype(F32)
        o = acc[:, :HEAD_DIM] / acc[:, HEAD_DIM:HEAD_DIM + 1]
        o_ref[:, j * HEAD_DIM:(j + 1) * HEAD_DIM] = (o * jax.nn.sigmoid(z)).astype(o_ref.dtype)


def _fox_attn(qk_hm, vz_hm, cum, *, B, S, blk=512):
    H, HP = N_HEADS, FOX_HEADS_PER_STEP
    G = H // HP
    T = B * S
    blk = min(blk, S)
    nq = S // blk
    return pl.pallas_call(
        functools.partial(_fox_attn_kernel, blk=blk), grid=(B * G, nq),
        in_specs=[pl.BlockSpec((HP, blk, LANES), lambda g, qi: (g % G, (g // G) * nq + qi, 0)),
                  pl.BlockSpec((HP, S, LANES), lambda g, qi: (G + g % G, g // G, 0)),
                  pl.BlockSpec((HP, S, LANES), lambda g, qi: (g % G, g // G, 0)),
                  pl.BlockSpec((HP, blk, LANES), lambda g, qi: (G + g % G, (g // G) * nq + qi, 0)),
                  pl.BlockSpec((S, LANES), lambda g, qi: (g // G, 0))],
        out_specs=pl.BlockSpec((blk, HP * LANES), lambda g, qi: ((g // G) * nq + qi, g % G)),
        out_shape=jax.ShapeDtypeStruct((T, H * LANES), BF16),
        scratch_shapes=[pltpu.VMEM((HP, S, 2 * HEAD_DIM), BF16), pltpu.VMEM((HP, S, 2 * HEAD_DIM), BF16)],
        name="fox_attn", compiler_params=_cparams("parallel", "arbitrary"))(qk_hm, qk_hm, vz_hm, vz_hm, cum)


def _forgetting_attention(xb, h, w_in, b_forget, q_norm_w, k_norm_w, w_out, gain, bias, *, B, S):
    W, H = N_HEADS * HEAD_DIM, N_HEADS
    norm_row = jnp.concatenate([jnp.tile(q_norm_w.astype(F32) * (HEAD_DIM ** -0.5 * LOG2E), H),
                                jnp.tile(k_norm_w.astype(F32), H)]).reshape(1, 2 * W)
    qk_hm = _proj(xb, w_in[:, :2 * W].astype(BF16), out_dtype=BF16, tm=1024, tn=512,
                  epilogue="rms", row=norm_row, head_major=True)
    vz_hm = _proj(xb, w_in[:, 2 * W:4 * W].astype(BF16), out_dtype=BF16, tm=1024, tn=512, head_major=True)
    w_f = jnp.pad(w_in[:, 4 * W:], ((0, 0), (0, LANES - H))).astype(BF16)
    b_row = jnp.pad(b_forget.astype(F32), (0, LANES - H)).reshape(1, LANES)
    cum = _fox_cum(xb, w_f, b_row, S=S)
    o = _fox_attn(qk_hm, vz_hm, cum, B=B, S=S)
    return _outproj_ln(o, w_out.astype(BF16), h, gain, bias)


def _lru_kernel(u_ref, y_ref, cw_ref, cb_ref, wa_ref, ba_ref, wx_ref, bx_ref, ap_ref, o_ref,
                ext, a_s, b_s, h_s, hcar, *, ts):
    t = pl.program_id(1)
    W = u_ref.shape[1]

    @pl.when(t == 0)
    def _():
        ext[0:SUBLANES, :] = jnp.zeros((SUBLANES, W), F32)
        hcar[...] = jnp.zeros_like(hcar)

    raw = u_ref[...]
    ext[SUBLANES:SUBLANES + ts, :] = raw
    u = cb_ref[...]
    for kk in range(CONV_WIDTH):
        start = SUBLANES - (CONV_WIDTH - 1) + kk
        u = u + ext[pl.ds(start, ts), :] * cw_ref[kk:kk + 1, :]
    ext[0:SUBLANES, :] = raw[ts - SUBLANES:, :]

    ra, rx = [], []
    for n in range(W // LANES):
        ub = u[:, n * LANES:(n + 1) * LANES].astype(BF16)
        ra.append(jnp.dot(ub, wa_ref[n], preferred_element_type=F32))
        rx.append(jnp.dot(ub, wx_ref[n], preferred_element_type=F32))
    r = jax.nn.sigmoid(jnp.concatenate(ra, axis=-1) + ba_ref[...])
    gate_x = jax.nn.sigmoid(jnp.concatenate(rx, axis=-1) + bx_ref[...])
    log_a = -LRU_C * r * jax.nn.softplus(-ap_ref[...])
    a_s[...] = jnp.exp(log_a)
    b_s[...] = jnp.sqrt(1.0 - jnp.exp(2.0 * log_a)) * (gate_x * u)

    rowi = lax.broadcasted_iota(jnp.int32, (SUBLANES, W), 0)

    def sub(j, hprev):
        off = pl.multiple_of(j * SUBLANES, SUBLANES)
        aa = a_s[pl.ds(off, SUBLANES), :]
        bb = b_s[pl.ds(off, SUBLANES), :]
        for d in (1, 2, 4):
            keep = rowi >= d
            a_sh = pltpu.roll(aa, d, 0)
            b_sh = pltpu.roll(bb, d, 0)
            bb = jnp.where(keep, aa * b_sh + bb, bb)
            aa = jnp.where(keep, aa * a_sh, aa)
        hh = aa * hprev + bb
        h_s[pl.ds(off, SUBLANES), :] = hh
        return hh[SUBLANES - 1:SUBLANES, :]

    hcar[...] = lax.fori_loop(0, ts // SUBLANES, sub, hcar[...])
    o_ref[...] = (h_s[...] * y_ref[...].astype(F32)).astype(o_ref.dtype)


def _lru_scan(u_raw, y, conv_w, conv_b, w_a, b_a, w_x, b_x, a_param, *, B, S, ts=256):
    T, W = u_raw.shape
    ts = min(ts, S)
    nt = S // ts
    nb = W // LANES
    tok = lambda b, t: (b * nt + t, 0)
    fixed2 = lambda b, t: (0, 0)
    fixed3 = lambda b, t: (0, 0, 0)
    return pl.pallas_call(
        functools.partial(_lru_kernel, ts=ts), grid=(B, nt),
        in_specs=[pl.BlockSpec((ts, W), tok), pl.BlockSpec((ts, W), tok),
                  pl.BlockSpec((CONV_WIDTH, W), fixed2), pl.BlockSpec((1, W), fixed2),
                  pl.BlockSpec((nb, LANES, LANES), fixed3), pl.BlockSpec((1, W), fixed2),
                  pl.BlockSpec((nb, LANES, LANES), fixed3), pl.BlockSpec((1, W), fixed2),
                  pl.BlockSpec((1, W), fixed2)],
        out_specs=pl.BlockSpec((ts, W), tok),
        out_shape=jax.ShapeDtypeStruct((T, W), BF16),
        scratch_shapes=[pltpu.VMEM((ts + SUBLANES, W), F32), pltpu.VMEM((ts, W), F32),
                        pltpu.VMEM((ts, W), F32), pltpu.VMEM((ts, W), F32), pltpu.VMEM((1, W), F32)],
        name="lru_scan", compiler_params=_cparams("parallel", "arbitrary"))(
            u_raw, y, conv_w, conv_b, w_a, b_a, w_x, b_x, a_param)


def _rglru_block(xb, h, w_in, conv_w, conv_b, w_gate_a, b_gate_a, w_gate_x, b_gate_x, a_param, w_out,
                 gain, bias, *, B, S):
    W = D_MODEL
    row = lambda v: v.astype(F32).reshape(1, W)
    y = _proj(xb, w_in[:, :W].astype(BF16), out_dtype=BF16, tm=1024, tn=512, epilogue="gelu")
    u_raw = _proj(xb, w_in[:, W:].astype(BF16), out_dtype=F32, tm=1024, tn=512)
    o = _lru_scan(u_raw, y, conv_w.astype(F32), row(conv_b), w_gate_a.astype(BF16), row(b_gate_a),
                  w_gate_x.astype(BF16), row(b_gate_x), row(a_param), B=B, S=S)
    return _outproj_ln(o, w_out.astype(BF16), h, gain, bias)


def _router_kernel(h_ref, w_ref, meta_ref, cnt_ref, carry):
    i = pl.program_id(0)

    @pl.when(i == 0)
    def _():
        carry[...] = jnp.zeros_like(carry)

    logits = jnp.dot(h_ref[...], w_ref[...], precision=HIGHEST, preferred_element_type=F32)
    tm = logits.shape[0]
    lane = lax.broadcasted_iota(jnp.int32, (tm, LANES), 1).astype(F32)
    first = lambda mask: jnp.min(jnp.where(mask, lane, float(LANES)), axis=-1, keepdims=True)

    is_g = lane < N_GROUPS
    gmax = jnp.max(jnp.where(is_g, logits, -jnp.inf), axis=-1, keepdims=True)
    gsum = jnp.sum(jnp.where(is_g, jnp.exp(logits - gmax), 0.0), axis=-1, keepdims=True)
    g_prob = 1.0 / gsum
    g_idx = first(is_g & (logits == gmax))
    lo = N_GROUPS + EXPERTS_PER_GROUP * g_idx
    is_e = (lane >= lo) & (lane < lo + EXPERTS_PER_GROUP)
    l1 = jnp.max(jnp.where(is_e, logits, -jnp.inf), axis=-1, keepdims=True)
    i1 = first(is_e & (logits == l1))
    is_e2 = is_e & (lane != i1)
    l2 = jnp.max(jnp.where(is_e2, logits, -jnp.inf), axis=-1, keepdims=True)
    i2 = first(is_e2 & (logits == l2))
    e2 = jnp.exp(l2 - l1)
    w1 = g_prob / (1.0 + e2)
    w2 = g_prob * e2 / (1.0 + e2)
    ex1, ex2 = i1 - N_GROUPS, i2 - N_GROUPS

    hot1 = lane == ex1
    hot2 = lane == ex2
    hot = jnp.where(hot1 | hot2, 1.0, 0.0)
    tri = (lax.broadcasted_iota(jnp.int32, (tm, tm), 0)
           > lax.broadcasted_iota(jnp.int32, (tm, tm), 1))
    prefix = jnp.dot(tri.astype(BF16), hot.astype(BF16), preferred_element_type=F32) + carry[...]
    rank1 = jnp.sum(jnp.where(hot1, prefix, 0.0), axis=-1, keepdims=True)
    rank2 = jnp.sum(jnp.where(hot2, prefix, 0.0), axis=-1, keepdims=True)
    total = carry[...] + jnp.sum(hot, axis=0, keepdims=True)
    carry[...] = total
    cnt_ref[...] = jnp.broadcast_to(total[None], cnt_ref.shape)

    meta = jnp.zeros((tm, LANES), F32)
    for idx, val in enumerate((ex1, ex2, w1, w2, rank1, rank2)):
        meta = jnp.where(lane == idx, val, meta)
    meta_ref[...] = meta


def _router(h, w_router, *, tm=512):
    T, D = h.shape
    tm = min(tm, T)
    nt = T // tm
    return pl.pallas_call(
        _router_kernel, grid=(nt,),
        in_specs=[pl.BlockSpec((tm, D), lambda i: (i, 0)),
                  pl.BlockSpec((D, LANES), lambda i: (0, 0))],
        out_specs=[pl.BlockSpec((tm, LANES), lambda i: (i, 0)),
                   pl.BlockSpec((1, SUBLANES, LANES), lambda i: (i, 0, 0))],
        out_shape=[jax.ShapeDtypeStruct((T, LANES), F32),
                   jax.ShapeDtypeStruct((nt, SUBLANES, LANES), F32)],
        scratch_shapes=[pltpu.VMEM((1, LANES), F32)],
        name="moe_router", compiler_params=_cparams("arbitrary"))(h, w_router)


def _pack_bf16_pairs(x):
    m = x.shape[1] // 2
    lo = pltpu.bitcast(x[:, :m].astype(BF16).astype(F32), U32)
    hi = pltpu.bitcast(x[:, m:].astype(BF16).astype(F32), U32)
    return (lo >> 16) | hi


def _unpack_bf16_pairs(p):
    return pltpu.bitcast(p << 16, F32), pltpu.bitcast(p & jnp.uint32(0xFFFF0000), F32)


SLAB_ROWS = D_MODEL // 2 // LANES


def _matrix_to_slabs(mat):
    chunks = jnp.stack([mat[:, j * LANES:(j + 1) * LANES] for j in range(SLAB_ROWS)])
    return pltpu.einshape("stl->tsl", chunks)


def _slabs_to_matrix(slabs):
    chunks = pltpu.einshape("tsl->stl", slabs)
    return jnp.concatenate([chunks[j] for j in range(SLAB_ROWS)], axis=-1)


def _dispatch_kernel(pos_ref, pad_ref, h_ref, xs_ref, slabs, zeros, sem, zsem, *, tile, n_tok):
    i = pl.program_id(0)
    base = i * tile
    cur = i % 2

    def pad_copy(e):
        return pltpu.make_async_copy(zeros, xs_ref.at[pl.ds(pad_ref[e], MOE_TILE)], zsem)

    def tail_copy(t):
        return pltpu.make_async_copy(zeros, xs_ref.at[pl.ds(t * MOE_TILE, MOE_TILE)], zsem)

    @pl.when(i == 0)
    def _():
        zeros[...] = jnp.zeros_like(zeros)
        n_tiles = xs_ref.shape[0] // MOE_TILE
        for e in range(N_EXPERTS):
            @pl.when(pad_ref[e] >= 0)
            def _():
                pad_copy(e).start()
        lax.fori_loop(pad_ref[N_EXPERTS], n_tiles, lambda t, c: (tail_copy(t).start(), c)[1], 0)
        for e in range(N_EXPERTS):
            @pl.when(pad_ref[e] >= 0)
            def _():
                pad_copy(e).wait()
        lax.fori_loop(pad_ref[N_EXPERTS], n_tiles, lambda t, c: (tail_copy(t).wait(), c)[1], 0)

    def wait_rows(buf):
        for _ in range(2):
            pltpu.make_async_copy(slabs.at[buf], xs_ref.at[pl.ds(0, tile)], sem.at[buf]).wait()

    slabs[cur] = _matrix_to_slabs(_pack_bf16_pairs(h_ref[...]))

    def issue(r, c):
        for slot in range(2):
            pltpu.make_async_copy(slabs.at[cur, r], xs_ref.at[pos_ref[slot * n_tok + base + r]],
                                  sem.at[cur]).start()
        return c

    lax.fori_loop(0, tile, issue, 0, unroll=8)

    @pl.when(i > 0)
    def _():
        wait_rows(1 - cur)

    @pl.when(i == pl.num_programs(0) - 1)
    def _():
        wait_rows(cur)


def _dispatch(pos, pad_start, h, *, n_rows, tile=256):
    T, D = h.shape
    tile = min(tile, T)
    return pl.pallas_call(
        functools.partial(_dispatch_kernel, tile=tile, n_tok=T),
        grid_spec=pltpu.PrefetchScalarGridSpec(
            num_scalar_prefetch=2, grid=(T // tile,),
            in_specs=[pl.BlockSpec((tile, D), lambda i, pos, pad: (i, 0))],
            out_specs=pl.BlockSpec(memory_space=pl.ANY),
            scratch_shapes=[pltpu.VMEM((2, tile, SLAB_ROWS, LANES), U32),
                            pltpu.VMEM((MOE_TILE, SLAB_ROWS, LANES), U32),
                            pltpu.SemaphoreType.DMA((2,)), pltpu.SemaphoreType.DMA]),
        out_shape=jax.ShapeDtypeStruct((n_rows, SLAB_ROWS, LANES), U32),
        name="moe_dispatch", compiler_params=_cparams("arbitrary"))(pos, pad_start, h)


def _experts_kernel(te_ref, na_ref, xs_ref, wgu_ref, wd_ref, y_ref, wgu_bf, wd_bf):
    i = pl.program_id(0)

    @pl.when(i >= na_ref[0])
    def _():
        y_ref[...] = jnp.zeros_like(y_ref)

    @pl.when(i < na_ref[0])
    def _():
        @pl.when((i == 0) | (te_ref[i] != te_ref[jnp.maximum(i - 1, 0)]))
        def _():
            wgu_bf[...] = wgu_ref[0].astype(BF16)
            wd_bf[...] = wd_ref[0].astype(BF16)

        lo, hi = _unpack_bf16_pairs(_slabs_to_matrix(xs_ref[...]))
        x = jnp.concatenate([lo, hi], axis=-1).astype(BF16)
        gu = jnp.dot(x, wgu_bf[...], preferred_element_type=F32)
        hid = jax.nn.silu(gu[:, :EXPERT_FF]) * gu[:, EXPERT_FF:]
        y = jnp.dot(hid.astype(BF16), wd_bf[...], preferred_element_type=F32)
        y_ref[...] = _matrix_to_slabs(_pack_bf16_pairs(y))


def _experts(tile_expert, n_active, xs, w_gate_up, w_down):
    n_rows = xs.shape[0]
    D = D_MODEL
    tm = MOE_TILE
    row_map = lambda i, te, na: (jnp.minimum(i, na[0] - 1), 0, 0)
    w_map = lambda i, te, na: (te[jnp.minimum(i, na[0] - 1)], 0, 0)
    return pl.pallas_call(
        _experts_kernel,
        grid_spec=pltpu.PrefetchScalarGridSpec(
            num_scalar_prefetch=2, grid=(n_rows // tm,),
            in_specs=[pl.BlockSpec((tm, SLAB_ROWS, LANES), row_map),
                      pl.BlockSpec((1, D, 2 * EXPERT_FF), w_map),
                      pl.BlockSpec((1, EXPERT_FF, D), w_map)],
            out_specs=pl.BlockSpec((tm, SLAB_ROWS, LANES), lambda i, te, na: (i, 0, 0)),
            scratch_shapes=[pltpu.VMEM((D, 2 * EXPERT_FF), BF16), pltpu.VMEM((EXPERT_FF, D), BF16)]),
        out_shape=jax.ShapeDtypeStruct((n_rows, SLAB_ROWS, LANES), U32),
        name="moe_experts", compiler_params=_cparams("arbitrary"))(tile_expert, n_active, xs, w_gate_up, w_down)


def _combine_ln_kernel(pos_ref, h_ref, meta_ref, g_ref, b_ref, ys_ref, o_ref, ob_ref, slabs, sem,
                       *, tile, n_tok):
    i = pl.program_id(0)
    n_steps = pl.num_programs(0)
    cur = i % 2

    def gather_tile(step, buf):
        def issue(r, c):
            for slot in range(2):
                pltpu.make_async_copy(ys_ref.at[pos_ref[slot * n_tok + step * tile + r]],
                                      slabs.at[buf, slot, r], sem.at[buf]).start()
            return c

        lax.fori_loop(0, tile, issue, 0, unroll=8)

    @pl.when(i == 0)
    def _():
        gather_tile(0, 0)

    @pl.when(i + 1 < n_steps)
    def _():
        gather_tile(i + 1, 1 - cur)

    for slot in range(2):
        pltpu.make_async_copy(ys_ref.at[pl.ds(0, tile)], slabs.at[cur, slot], sem.at[cur]).wait()

    meta = meta_ref[...]
    halves = [None, None]
    for slot in range(2):
        lo, hi = _unpack_bf16_pairs(_slabs_to_matrix(slabs[cur, slot]))
        wgt = meta[:, 2 + slot:3 + slot]
        halves = [wgt * part if acc is None else acc + wgt * part
                  for acc, part in zip(halves, (lo, hi))]
    ffn = jnp.concatenate(halves, axis=-1)
    out = _layer_norm(ALPHA * h_ref[...] + ffn, g_ref[...], b_ref[...])
    o_ref[...] = out
    ob_ref[...] = out.astype(BF16)


def _combine_ln(pos, h, meta, gain, bias, ys, *, tile=256):
    T, D = h.shape
    tile = min(tile, T)
    tok = lambda i, pos: (i, 0)
    fixed = lambda i, pos: (0, 0)
    return pl.pallas_call(
        functools.partial(_combine_ln_kernel, tile=tile, n_tok=T),
        grid_spec=pltpu.PrefetchScalarGridSpec(
            num_scalar_prefetch=1, grid=(T // tile,),
            in_specs=[pl.BlockSpec((tile, D), tok), pl.BlockSpec((tile, LANES), tok),
                      pl.BlockSpec((1, D), fixed), pl.BlockSpec((1, D), fixed),
                      pl.BlockSpec(memory_space=pl.ANY)],
            out_specs=[pl.BlockSpec((tile, D), tok), pl.BlockSpec((tile, D), tok)],
            scratch_shapes=[pltpu.VMEM((2, 2, tile, SLAB_ROWS, LANES), U32), pltpu.SemaphoreType.DMA((2,))]),
        out_shape=[jax.ShapeDtypeStruct((T, D), F32), jax.ShapeDtypeStruct((T, D), BF16)],
        name="moe_combine_ln", compiler_params=_cparams("arbitrary"))(pos, h, meta, gain, bias, ys)


def _hierarchical_moe(h, w_router_group, w_router_expert, w_gate_up, w_down, gain, bias):
    T, D = h.shape
    w_router = jnp.concatenate(
        [w_router_group, jnp.transpose(w_router_expert, (1, 0, 2)).reshape(D, N_EXPERTS)], axis=1)
    w_router = jnp.pad(w_router.astype(F32), ((0, 0), (0, LANES - N_GROUPS - N_EXPERTS)))
    meta, counts = _router(h, w_router)

    cnt = counts[-1, 0, :N_EXPERTS].astype(jnp.int32)
    padded = (cnt + MOE_TILE - 1) // MOE_TILE * MOE_TILE
    ends = jnp.cumsum(padded)
    starts = ends - padded
    ex = meta[:, 0:2].astype(jnp.int32)
    rank = meta[:, 4:6].astype(jnp.int32)
    pos = (jnp.take(starts, ex) + rank).T.reshape(2 * T)
    n_rows = 2 * T + N_EXPERTS * MOE_TILE
    n_tiles = n_rows // MOE_TILE
    tile_start = jnp.arange(n_tiles, dtype=jnp.int32) * MOE_TILE
    tile_expert = jnp.minimum(jnp.sum(ends[None, :] <= tile_start[:, None], axis=1), N_EXPERTS - 1).astype(jnp.int32)
    n_active = (ends[-1:] // MOE_TILE).astype(jnp.int32)
    pad_start = jnp.concatenate([jnp.where(padded > 0, ends - MOE_TILE, -1).astype(jnp.int32), n_active])

    xs = _dispatch(pos, pad_start, h, n_rows=n_rows)
    ys = _experts(tile_expert, n_active, xs, w_gate_up, w_down)
    return _combine_ln(pos, h, meta, gain, bias, ys)


def kernel(x, ln_gain, ln_bias, gdn_w_in, gdn_conv_w, gdn_a_log, gdn_dt_bias, gdn_norm_w, gdn_w_out, fox_w_in, fox_b_forget, fox_q_norm_w, fox_k_norm_w, fox_w_out, lru_w_in, lru_conv_w, lru_conv_b, lru_w_gate_a, lru_b_gate_a, lru_w_gate_x, lru_b_gate_x, lru_a_param, lru_w_out, moe_w_router_group, moe_w_router_expert, moe_w_gate_up, moe_w_down):
    B, S, D = x.shape
    h = x.reshape(B * S, D).astype(F32)
    hb = h.astype(BF16)
    row = lambda v: v.astype(F32).reshape(1, D)
    for layer in range(DEPTH):
        kind, j = layer % 3, layer // 3
        g0, b0 = row(ln_gain[layer, 0]), row(ln_bias[layer, 0])
        g1, b1 = row(ln_gain[layer, 1]), row(ln_bias[layer, 1])
        if kind == 0:
            h = _gated_deltanet(hb, h, gdn_w_in[j], gdn_conv_w[j], gdn_a_log[j], gdn_dt_bias[j],
                                gdn_norm_w[j], gdn_w_out[j], g0, b0, B=B, S=S)
        elif kind == 1:
            h = _forgetting_attention(hb, h, fox_w_in[j], fox_b_forget[j], fox_q_norm_w[j],
                                      fox_k_norm_w[j], fox_w_out[j], g0, b0, B=B, S=S)
        else:
            h = _rglru_block(hb, h, lru_w_in[j], lru_conv_w[j], lru_conv_b[j], lru_w_gate_a[j],
                             lru_b_gate_a[j], lru_w_gate_x[j], lru_b_gate_x[j], lru_a_param[j],
                             lru_w_out[j], g0, b0, B=B, S=S)
        h, hb = _hierarchical_moe(h, moe_w_router_group[layer], moe_w_router_expert[layer],
                                  moe_w_gate_up[layer], moe_w_down[layer], g1, b1)
    return h.reshape(B, S, D).astype(x.dtype)
```

```python
import functools
import math

import jax
import jax.numpy as jnp
from jax import lax
from jax.experimental import pallas as pl
from jax.experimental.pallas import tpu as pltpu

F32 = jnp.float32
BF16 = jnp.bfloat16
U32 = jnp.uint32
HIGHEST = lax.Precision.HIGHEST

D_MODEL = 2048
N_HEADS = 16
HEAD_DIM = 128
LANES = 128
SUBLANES = 8
CONV_WIDTH = 4
GDN_CHUNK = 128
GDN_INV_BLOCK = 64
LRU_C = 8.0
N_GROUPS = 4
EXPERTS_PER_GROUP = 8
N_EXPERTS = N_GROUPS * EXPERTS_PER_GROUP
EXPERT_FF = D_MODEL // 4
DEPTH = 4
ALPHA = (2 * DEPTH) ** 0.25
LN_EPS = 1e-5
NORM_EPS = 1e-6
NEG_BIG = -1e30
LOG2E = math.log2(math.e)

MOE_TILE = 256
VMEM_LIMIT = 56 * 1024 * 1024


def _cparams(*sem):
    return pltpu.CompilerParams(dimension_semantics=sem, vmem_limit_bytes=VMEM_LIMIT)


def _bdot(a, b):
    return jnp.dot(a.astype(BF16), b.astype(BF16), preferred_element_type=F32)


def _layer_norm(y, gain, bias):
    mu = jnp.mean(y, axis=-1, keepdims=True)
    yc = y - mu
    var = jnp.mean(yc * yc, axis=-1, keepdims=True)
    return yc * lax.rsqrt(var + LN_EPS) * gain + bias


def _proj_kernel(x_ref, w_ref, *rest, epilogue, head_major):
    o_ref, w_bf = rest[-2], rest[-1]

    @pl.when(pl.program_id(1) == 0)
    def _():
        w_bf[...] = w_ref[...].astype(BF16)

    acc = jnp.dot(x_ref[...], w_bf[...], preferred_element_type=F32)
    nb = acc.shape[1] // LANES
    if epilogue == "gelu":
        acc = jax.nn.gelu(acc, approximate=True)
    for j in range(nb) if (head_major or epilogue == "rms") else ():
        blk = acc[:, j * LANES:(j + 1) * LANES]
        if epilogue == "rms":
            ms = jnp.mean(blk * blk, axis=-1, keepdims=True)
            blk = blk * lax.rsqrt(ms + NORM_EPS) * rest[0][:, j * LANES:(j + 1) * LANES]
        if head_major:
            o_ref[j] = blk.astype(o_ref.dtype)
        else:
            o_ref[:, j * LANES:(j + 1) * LANES] = blk.astype(o_ref.dtype)
    if not (head_major or epilogue == "rms"):
        o_ref[...] = acc.astype(o_ref.dtype)


def _proj(x, w, *, col0, n_cols, out_dtype, tm=1024, tn=1024, epilogue=None, row=None, head_major=False):
    T, K = x.shape
    tm, tn = min(tm, T), min(tn, n_cols)
    assert col0 % tn == 0 and n_cols % tn == 0
    in_specs = [pl.BlockSpec((tm, K), lambda j, i: (i, 0)),
                pl.BlockSpec((K, tn), lambda j, i: (0, col0 // tn + j))]
    args = [x, w]
    if row is not None:
        in_specs.append(pl.BlockSpec((1, tn), lambda j, i: (0, j)))
        args.append(row)
    if head_major:
        out_shape = jax.ShapeDtypeStruct((n_cols // LANES, T, LANES), out_dtype)
        out_spec = pl.BlockSpec((tn // LANES, tm, LANES), lambda j, i: (j, i, 0))
    else:
        out_shape = jax.ShapeDtypeStruct((T, n_cols), out_dtype)
        out_spec = pl.BlockSpec((tm, tn), lambda j, i: (i, j))
    return pl.pallas_call(
        functools.partial(_proj_kernel, epilogue=epilogue, head_major=head_major),
        grid=(n_cols // tn, T // tm), in_specs=in_specs, out_specs=out_spec, out_shape=out_shape,
        scratch_shapes=[pltpu.VMEM((K, tn), BF16)],
        name="proj" + ("_" + epilogue if epilogue else ""),
        compiler_params=_cparams("parallel", "arbitrary"))(*args)


def _outproj_ln_kernel(a_ref, w_ref, h_ref, g_ref, b_ref, o_ref):
    mix = jnp.dot(a_ref[...], w_ref[...], preferred_element_type=F32)
    o_ref[...] = _layer_norm(ALPHA * h_ref[...] + mix, g_ref[...], b_ref[...])


def _outproj_ln(a, w, h, gain, bias, *, tm=512):
    T, W = a.shape
    D = w.shape[1]
    tm = min(tm, T)
    return pl.pallas_call(
        _outproj_ln_kernel, grid=(T // tm,),
        in_specs=[pl.BlockSpec((tm, W), lambda i: (i, 0)),
                  pl.BlockSpec((W, D), lambda i: (0, 0)),
                  pl.BlockSpec((tm, D), lambda i: (i, 0)),
                  pl.BlockSpec((1, D), lambda i: (0, 0)),
                  pl.BlockSpec((1, D), lambda i: (0, 0))],
        out_specs=pl.BlockSpec((tm, D), lambda i: (i, 0)),
        out_shape=jax.ShapeDtypeStruct((T, D), F32),
        name="outproj_ln", compiler_params=_cparams("parallel"))(a, w, h, gain, bias)


def _gdn_gates_kernel(x_ref, w_ref, alog_ref, dt_ref, o_ref):
    acc = jnp.dot(x_ref[...], w_ref[...], preferred_element_type=F32)
    lane = lax.broadcasted_iota(jnp.int32, acc.shape, 1)
    beta = jax.nn.sigmoid(acc)
    log_decay = -jnp.exp(alog_ref[...]) * jax.nn.softplus(acc + dt_ref[...])
    o_ref[...] = jnp.where(lane < N_HEADS, beta, log_decay)


def _gdn_gates(xb, w_ba, alog_row, dt_row, *, tm=1024):
    T, K = xb.shape
    tm = min(tm, T)
    return pl.pallas_call(
        _gdn_gates_kernel, grid=(T // tm,),
        in_specs=[pl.BlockSpec((tm, K), lambda i: (i, 0)),
                  pl.BlockSpec((K, LANES), lambda i: (0, 0)),
                  pl.BlockSpec((1, LANES), lambda i: (0, 0)),
                  pl.BlockSpec((1, LANES), lambda i: (0, 0))],
        out_specs=pl.BlockSpec((tm, LANES), lambda i: (i, 0)),
        out_shape=jax.ShapeDtypeStruct((T, LANES), F32),
        name="gdn_gates", compiler_params=_cparams("parallel"))(xb, w_ba, alog_row, dt_row)


def _bmm(a, b):
    return jnp.einsum("hcm,hmd->hcd", a.astype(BF16), b.astype(BF16), preferred_element_type=F32)


def _bmm_nt(a, b):
    return jnp.einsum("hcd,hmd->hcm", a.astype(BF16), b.astype(BF16), preferred_element_type=F32)


def _gdn_chunk_kernel(q_ref, k_ref, v_ref, z_ref, gt_ref, cw_ref, nw_ref, o_ref, state, ext, *, C):
    H = N_HEADS
    c = pl.program_id(1)

    @pl.when(c == 0)
    def _():
        state[...] = jnp.zeros_like(state)
        ext[:, :, 0:SUBLANES, :] = jnp.zeros((3, H, SUBLANES, LANES), F32)

    acts = []
    for i, ref in enumerate((q_ref, k_ref, v_ref)):
        raw = ref[...].astype(F32)
        ext[i, :, SUBLANES:SUBLANES + C, :] = raw
        acc = None
        for kk in range(CONV_WIDTH):
            start = SUBLANES - (CONV_WIDTH - 1) + kk
            term = ext[i, :, pl.ds(start, C), :] * cw_ref[kk, i * H:(i + 1) * H]
            acc = term if acc is None else acc + term
        ext[i, :, 0:SUBLANES, :] = raw[:, C - SUBLANES:, :]
        acts.append(acc * jax.nn.sigmoid(acc))
    qc, kc, v = acts
    q = qc * lax.rsqrt(jnp.sum(qc * qc, axis=-1, keepdims=True) + NORM_EPS) * (HEAD_DIM ** -0.5)
    k = kc * lax.rsqrt(jnp.sum(kc * kc, axis=-1, keepdims=True) + NORM_EPS)

    gt = gt_ref[...]
    ri = lax.broadcasted_iota(jnp.int32, (C, C), 0)
    ci = lax.broadcasted_iota(jnp.int32, (C, C), 1)
    causal = ri >= ci
    strict = ri > ci
    g_all = jnp.dot(causal.astype(F32), gt, precision=HIGHEST, preferred_element_type=F32)
    sel = (lax.broadcasted_iota(jnp.int32, (H, LANES), 1)
           == lax.broadcasted_iota(jnp.int32, (H, LANES), 0) + H).astype(F32)
    g_t = lax.dot_general(sel, g_all, (((1,), (1,)), ((), ())), precision=HIGHEST,
                          preferred_element_type=F32)
    g_col = jnp.stack([g_all[:, H + h:H + h + 1] for h in range(H)])
    g_row = jnp.stack([g_t[h:h + 1, :] for h in range(H)])
    beta = jnp.stack([gt[:, h:h + 1] for h in range(H)])
    decay = jnp.exp(jnp.where(causal[None], g_col - g_row, -jnp.inf))
    exp_g = jnp.exp(g_col)
    g_last = g_col[:, C - 1:C, :]

    kb = k * beta
    lower = jnp.where(strict[None], _bmm_nt(kb, k) * decay, 0.0)
    eye = (ri == ci).astype(F32)[None]
    same_block = (ri // GDN_INV_BLOCK == ci // GDN_INV_BLOCK)[None]
    m = jnp.where(same_block, -lower, 0.0)
    tinv = eye + m
    for _ in range(int(math.log2(min(C, GDN_INV_BLOCK))) - 1):
        m = _bmm(m, m)
        tinv = tinv + _bmm(tinv, m)
    if C > GDN_INV_BLOCK:
        assert C == 2 * GDN_INV_BLOCK
        off_block = jnp.where(same_block, 0.0, lower)
        tinv = tinv - _bmm(_bmm(tinv, off_block), tinv)
    rhs = jnp.concatenate([v * beta, kb * exp_g], axis=-1)
    sol = _bmm(tinv, rhs)
    u, w = sol[..., :HEAD_DIM], sol[..., HEAD_DIM:]
    attn = jnp.where(causal[None], _bmm_nt(q, k) * decay, 0.0)

    s = state[...]
    v_new = u - _bmm(w, s)
    o = _bmm(q * exp_g, s) + _bmm(attn, v_new)
    k_dec = k * jnp.exp(g_last - g_col)
    state[...] = s * jnp.exp(g_last) + jnp.einsum(
        "hcd,hce->hde", k_dec.astype(BF16), v_new.astype(BF16), preferred_element_type=F32)

    o = o * lax.rsqrt(jnp.mean(o * o, axis=-1, keepdims=True) + NORM_EPS) * nw_ref[...]
    z = z_ref[...].astype(F32)
    o = o * (z * jax.nn.sigmoid(z))
    for h in range(H):
        o_ref[:, h * LANES:(h + 1) * LANES] = o[h].astype(o_ref.dtype)


def _gdn_chunk(proj_hm, gates, conv_w, norm_w, *, B, S, C=GDN_CHUNK):
    H = N_HEADS
    T = B * S
    nc = S // C
    slab = lambda j: pl.BlockSpec((H, C, LANES), lambda b, c: (j, b * nc + c, 0))
    return pl.pallas_call(
        functools.partial(_gdn_chunk_kernel, C=C), grid=(B, nc),
        in_specs=[slab(0), slab(1), slab(2), slab(3),
                  pl.BlockSpec((C, LANES), lambda b, c: (b * nc + c, 0)),
                  pl.BlockSpec((CONV_WIDTH, 3 * H, 1, LANES), lambda b, c: (0, 0, 0, 0)),
                  pl.BlockSpec((1, 1, LANES), lambda b, c: (0, 0, 0))],
        out_specs=pl.BlockSpec((C, H * LANES), lambda b, c: (b * nc + c, 0)),
        out_shape=jax.ShapeDtypeStruct((T, H * LANES), BF16),
        scratch_shapes=[pltpu.VMEM((H, HEAD_DIM, HEAD_DIM), F32),
                        pltpu.VMEM((3, H, C + SUBLANES, LANES), F32)],
        name="gdn_chunk", compiler_params=_cparams("parallel", "arbitrary"))(
            proj_hm, proj_hm, proj_hm, proj_hm, gates, conv_w, norm_w)


def _gated_deltanet(xb, h, w_in, conv_w, a_log, dt_bias, norm_w, w_out, gain, bias, *, B, S):
    W, H = N_HEADS * HEAD_DIM, N_HEADS
    proj_hm = _proj(xb, w_in, col0=0, n_cols=4 * W, out_dtype=BF16, head_major=True)
    w_ba = jnp.pad(w_in[:, 4 * W:], ((0, 0), (0, LANES - 2 * H))).astype(BF16)
    lane_pad = lambda v: jnp.pad(v.astype(F32), (H, LANES - 2 * H)).reshape(1, LANES)
    gates = _gdn_gates(xb, w_ba, lane_pad(a_log), lane_pad(dt_bias))
    o = _gdn_chunk(proj_hm, gates, conv_w.reshape(CONV_WIDTH, 3 * H, 1, LANES).astype(F32),
                   norm_w.reshape(1, 1, LANES).astype(F32), B=B, S=S)
    return _outproj_ln(o, w_out.astype(BF16), h, gain, bias)


def _fox_cum_kernel(x_ref, w_ref, b_ref, o_ref, carry, *, tiles_per_seq):
    i = pl.program_id(0)

    @pl.when(i % tiles_per_seq == 0)
    def _():
        carry[...] = jnp.zeros_like(carry)

    acc = jnp.dot(x_ref[...], w_ref[...], preferred_element_type=F32)
    log_f = jax.nn.log_sigmoid(acc + b_ref[...])
    tg = acc.shape[0]
    tri = (lax.broadcasted_iota(jnp.int32, (tg, tg), 0)
           >= lax.broadcasted_iota(jnp.int32, (tg, tg), 1)).astype(F32)
    cum = jnp.dot(tri, log_f, precision=HIGHEST, preferred_element_type=F32) + carry[...]
    carry[...] = cum[tg - 1:tg, :]
    o_ref[...] = cum


def _fox_cum(xb, w_f, b_row, *, S, tg=512):
    T, K = xb.shape
    tg = min(tg, S)
    return pl.pallas_call(
        functools.partial(_fox_cum_kernel, tiles_per_seq=S // tg), grid=(T // tg,),
        in_specs=[pl.BlockSpec((tg, K), lambda i: (i, 0)),
                  pl.BlockSpec((K, LANES), lambda i: (0, 0)),
                  pl.BlockSpec((1, LANES), lambda i: (0, 0))],
        out_specs=pl.BlockSpec((tg, LANES), lambda i: (i, 0)),
        out_shape=jax.ShapeDtypeStruct((T, LANES), F32),
        scratch_shapes=[pltpu.VMEM((1, LANES), F32)],
        name="fox_cum", compiler_params=_cparams("arbitrary"))(xb, w_f, b_row)


def _split3_bf16(x):
    hi = x.astype(BF16).astype(F32)
    mid = (x - hi).astype(BF16).astype(F32)
    lo = (x - hi - mid).astype(BF16).astype(F32)
    return hi, mid, lo


FOX_HEADS_PER_STEP = 4


def _fox_attn_kernel(q_ref, k_ref, v_ref, z_ref, cum_ref, o_ref, k_aug, v_aug, *, blk):
    HP = FOX_HEADS_PER_STEP
    group = pl.program_id(0) % (N_HEADS // HP)
    qi = pl.program_id(1)
    S = k_ref.shape[1]

    @pl.when(qi == 0)
    def _():
        lane_row = lax.broadcasted_iota(jnp.int32, (1, LANES), 1)
        lane = lax.broadcasted_iota(jnp.int32, (S, LANES), 1)
        for j in range(HP):
            pick = (lane_row == group * HP + j).astype(F32)
            bias = -LOG2E * jnp.sum(cum_ref[...] * pick, axis=-1, keepdims=True)
            hi, mid, lo = _split3_bf16(bias)
            k_aug[j, :, :HEAD_DIM] = k_ref[j]
            k_aug[j, :, HEAD_DIM:] = jnp.where(
                lane == 0, hi, jnp.where(lane == 1, mid, jnp.where(lane == 2, lo, 0.0))).astype(BF16)
            v_aug[j, :, :HEAD_DIM] = v_ref[j]
            v_aug[j, :, HEAD_DIM:] = jnp.where(lane == 0, 1.0, 0.0).astype(BF16)

    lane_q = lax.broadcasted_iota(jnp.int32, (blk, LANES), 1)
    ones3 = jnp.where(lane_q < 3, 1.0, 0.0).astype(BF16)
    q = [jnp.concatenate([q_ref[j], ones3], axis=-1) for j in range(HP)]

    def step(kb, carry, masked):
        start = pl.multiple_of(kb * blk, blk)
        out = []
        for j in range(HP):
            m, acc = carry[j]
            s = lax.dot_general(q[j], k_aug[j, pl.ds(start, blk), :], (((1,), (1,)), ((), ())),
                                preferred_element_type=F32)
            if masked:
                keep = (lax.broadcasted_iota(jnp.int32, (blk, blk), 0)
                        >= lax.broadcasted_iota(jnp.int32, (blk, blk), 1))
                s = jnp.where(keep, s, NEG_BIG)
            m_new = jnp.maximum(m, jnp.max(s, axis=-1, keepdims=True))
            p = jnp.exp2(s - m_new)
            acc = jnp.exp2(m - m_new) * acc + jnp.dot(p.astype(BF16), v_aug[j, pl.ds(start, blk), :],
                                                      preferred_element_type=F32)
            out.append((m_new, acc))
        return tuple(out)

    init = tuple((jnp.full((blk, 1), NEG_BIG, F32), jnp.zeros((blk, 2 * HEAD_DIM), F32)) for _ in range(HP))
    carry = lax.fori_loop(0, qi, functools.partial(step, masked=False), init)
    carry = step(qi, carry, masked=True)
    for j in range(HP):
        acc = carry[j][1]
        z = z_ref[j].astype(F32)
        o = acc[:, :HEAD_DIM] / acc[:, HEAD_DIM:HEAD_DIM + 1]
        o_ref[:, j * HEAD_DIM:(j + 1) * HEAD_DIM] = (o * jax.nn.sigmoid(z)).astype(o_ref.dtype)


def _fox_attn(qk_hm, vz_hm, cum, *, B, S, blk=512):
    H, HP = N_HEADS, FOX_HEADS_PER_STEP
    G = H // HP
    T = B * S
    blk = min(blk, S)
    nq = S // blk
    return pl.pallas_call(
        functools.partial(_fox_attn_kernel, blk=blk), grid=(B * G, nq),
        in_specs=[pl.BlockSpec((HP, blk, LANES), lambda g, qi: (g % G, (g // G) * nq + qi, 0)),
                  pl.BlockSpec((HP, S, LANES), lambda g, qi: (G + g % G, g // G, 0)),
                  pl.BlockSpec((HP, S, LANES), lambda g, qi: (g % G, g // G, 0)),
                  pl.BlockSpec((HP, blk, LANES), lambda g, qi: (G + g % G, (g // G) * nq + qi, 0)),
                  pl.BlockSpec((S, LANES), lambda g, qi: (g // G, 0))],
        out_specs=pl.BlockSpec((blk, HP * LANES), lambda g, qi: ((g // G) * nq + qi, g % G)),
        out_shape=jax.ShapeDtypeStruct((T, H * LANES), BF16),
        scratch_shapes=[pltpu.VMEM((HP, S, 2 * HEAD_DIM), BF16), pltpu.VMEM((HP, S, 2 * HEAD_DIM), BF16)],
        name="fox_attn", compiler_params=_cparams("parallel", "arbitrary"))(qk_hm, qk_hm, vz_hm, vz_hm, cum)


def _forgetting_attention(xb, h, w_in, b_forget, q_norm_w, k_norm_w, w_out, gain, bias, *, B, S):
    W, H = N_HEADS * HEAD_DIM, N_HEADS
    norm_row = jnp.concatenate([jnp.tile(q_norm_w.astype(F32) * (HEAD_DIM ** -0.5 * LOG2E), H),
                                jnp.tile(k_norm_w.astype(F32), H)]).reshape(1, 2 * W)
    qk_hm = _proj(xb, w_in, col0=0, n_cols=2 * W, out_dtype=BF16, epilogue="rms", row=norm_row,
                  head_major=True)
    vz_hm = _proj(xb, w_in, col0=2 * W, n_cols=2 * W, out_dtype=BF16, head_major=True)
    w_f = jnp.pad(w_in[:, 4 * W:], ((0, 0), (0, LANES - H))).astype(BF16)
    b_row = jnp.pad(b_forget.astype(F32), (0, LANES - H)).reshape(1, LANES)
    cum = _fox_cum(xb, w_f, b_row, S=S)
    o = _fox_attn(qk_hm, vz_hm, cum, B=B, S=S)
    return _outproj_ln(o, w_out.astype(BF16), h, gain, bias)


def _lru_kernel(u_ref, y_ref, cw_ref, cb_ref, wa_ref, ba_ref, wx_ref, bx_ref, ap_ref, o_ref,
                ext, a_s, b_s, h_s, hcar, *, ts):
    t = pl.program_id(1)
    W = u_ref.shape[1]

    @pl.when(t == 0)
    def _():
        ext[0:SUBLANES, :] = jnp.zeros((SUBLANES, W), F32)
        hcar[...] = jnp.zeros_like(hcar)

    raw = u_ref[...]
    ext[SUBLANES:SUBLANES + ts, :] = raw
    u = cb_ref[...]
    for kk in range(CONV_WIDTH):
        start = SUBLANES - (CONV_WIDTH - 1) + kk
        u = u + ext[pl.ds(start, ts), :] * cw_ref[kk:kk + 1, :]
    ext[0:SUBLANES, :] = raw[ts - SUBLANES:, :]

    ra, rx = [], []
    for n in range(W // LANES):
        ub = u[:, n * LANES:(n + 1) * LANES].astype(BF16)
        ra.append(jnp.dot(ub, wa_ref[n], preferred_element_type=F32))
        rx.append(jnp.dot(ub, wx_ref[n], preferred_element_type=F32))
    r = jax.nn.sigmoid(jnp.concatenate(ra, axis=-1) + ba_ref[...])
    gate_x = jax.nn.sigmoid(jnp.concatenate(rx, axis=-1) + bx_ref[...])
    log_a = -LRU_C * r * jax.nn.softplus(-ap_ref[...])
    a_s[...] = jnp.exp(log_a)
    b_s[...] = jnp.sqrt(1.0 - jnp.exp(2.0 * log_a)) * (gate_x * u)

    rowi = lax.broadcasted_iota(jnp.int32, (SUBLANES, W), 0)

    def sub(j, hprev):
        off = pl.multiple_of(j * SUBLANES, SUBLANES)
        aa = a_s[pl.ds(off, SUBLANES), :]
        bb = b_s[pl.ds(off, SUBLANES), :]
        for d in (1, 2, 4):
            keep = rowi >= d
            a_sh = pltpu.roll(aa, d, 0)
            b_sh = pltpu.roll(bb, d, 0)
            bb = jnp.where(keep, aa * b_sh + bb, bb)
            aa = jnp.where(keep, aa * a_sh, aa)
        hh = aa * hprev + bb
        h_s[pl.ds(off, SUBLANES), :] = hh
        return hh[SUBLANES - 1:SUBLANES, :]

    hcar[...] = lax.fori_loop(0, ts // SUBLANES, sub, hcar[...])
    o_ref[...] = (h_s[...] * y_ref[...].astype(F32)).astype(o_ref.dtype)


def _lru_scan(u_raw, y, conv_w, conv_b, w_a, b_a, w_x, b_x, a_param, *, B, S, ts=256):
    T, W = u_raw.shape
    ts = min(ts, S)
    nt = S // ts
    nb = W // LANES
    tok = lambda b, t: (b * nt + t, 0)
    fixed2 = lambda b, t: (0, 0)
    fixed3 = lambda b, t: (0, 0, 0)
    return pl.pallas_call(
        functools.partial(_lru_kernel, ts=ts), grid=(B, nt),
        in_specs=[pl.BlockSpec((ts, W), tok), pl.BlockSpec((ts, W), tok),
                  pl.BlockSpec((CONV_WIDTH, W), fixed2), pl.BlockSpec((1, W), fixed2),
                  pl.BlockSpec((nb, LANES, LANES), fixed3), pl.BlockSpec((1, W), fixed2),
                  pl.BlockSpec((nb, LANES, LANES), fixed3), pl.BlockSpec((1, W), fixed2),
                  pl.BlockSpec((1, W), fixed2)],
        out_specs=pl.BlockSpec((ts, W), tok),
        out_shape=jax.ShapeDtypeStruct((T, W), BF16),
        scratch_shapes=[pltpu.VMEM((ts + SUBLANES, W), F32), pltpu.VMEM((ts, W), F32),
                        pltpu.VMEM((ts, W), F32), pltpu.VMEM((ts, W), F32), pltpu.VMEM((1, W), F32)],
        name="lru_scan", compiler_params=_cparams("parallel", "arbitrary"))(
            u_raw, y, conv_w, conv_b, w_a, b_a, w_x, b_x, a_param)


def _rglru_block(xb, h, w_in, conv_w, conv_b, w_gate_a, b_gate_a, w_gate_x, b_gate_x, a_param, w_out,
                 gain, bias, *, B, S):
    W = D_MODEL
    row = lambda v: v.astype(F32).reshape(1, W)
    y = _proj(xb, w_in, col0=0, n_cols=W, out_dtype=BF16, epilogue="gelu")
    u_raw = _proj(xb, w_in, col0=W, n_cols=W, out_dtype=F32)
    o = _lru_scan(u_raw, y, conv_w.astype(F32), row(conv_b), w_gate_a.astype(BF16), row(b_gate_a),
                  w_gate_x.astype(BF16), row(b_gate_x), row(a_param), B=B, S=S)
    return _outproj_ln(o, w_out.astype(BF16), h, gain, bias)


def _router_kernel(h_ref, whi_ref, wlo_ref, meta_ref, cnt_ref, carry):
    i = pl.program_id(0)

    @pl.when(i == 0)
    def _():
        carry[...] = jnp.zeros_like(carry)

    h = h_ref[...]
    h_hi = h.astype(BF16)
    h_lo = (h - h_hi.astype(F32)).astype(BF16)
    logits = (jnp.dot(h_hi, whi_ref[...], preferred_element_type=F32)
              + jnp.dot(h_hi, wlo_ref[...], preferred_element_type=F32)
              + jnp.dot(h_lo, whi_ref[...], preferred_element_type=F32))
    tm = logits.shape[0]
    lane = lax.broadcasted_iota(jnp.int32, (tm, LANES), 1).astype(F32)
    first = lambda mask: jnp.min(jnp.where(mask, lane, float(LANES)), axis=-1, keepdims=True)

    is_g = lane < N_GROUPS
    gmax = jnp.max(jnp.where(is_g, logits, -jnp.inf), axis=-1, keepdims=True)
    gsum = jnp.sum(jnp.where(is_g, jnp.exp(logits - gmax), 0.0), axis=-1, keepdims=True)
    g_prob = 1.0 / gsum
    g_idx = first(is_g & (logits == gmax))
    lo = N_GROUPS + EXPERTS_PER_GROUP * g_idx
    is_e = (lane >= lo) & (lane < lo + EXPERTS_PER_GROUP)
    l1 = jnp.max(jnp.where(is_e, logits, -jnp.inf), axis=-1, keepdims=True)
    i1 = first(is_e & (logits == l1))
    is_e2 = is_e & (lane != i1)
    l2 = jnp.max(jnp.where(is_e2, logits, -jnp.inf), axis=-1, keepdims=True)
    i2 = first(is_e2 & (logits == l2))
    e2 = jnp.exp(l2 - l1)
    w1 = g_prob / (1.0 + e2)
    w2 = g_prob * e2 / (1.0 + e2)
    ex1, ex2 = i1 - N_GROUPS, i2 - N_GROUPS

    hot1 = lane == ex1
    hot2 = lane == ex2
    hot = jnp.where(hot1 | hot2, 1.0, 0.0)
    tri = (lax.broadcasted_iota(jnp.int32, (tm, tm), 0)
           > lax.broadcasted_iota(jnp.int32, (tm, tm), 1))
    prefix = jnp.dot(tri.astype(BF16), hot.astype(BF16), preferred_element_type=F32) + carry[...]
    rank1 = jnp.sum(jnp.where(hot1, prefix, 0.0), axis=-1, keepdims=True)
    rank2 = jnp.sum(jnp.where(hot2, prefix, 0.0), axis=-1, keepdims=True)
    total = carry[...] + jnp.sum(hot, axis=0, keepdims=True)
    carry[...] = total
    cnt_ref[...] = jnp.broadcast_to(total[None], cnt_ref.shape)

    meta = jnp.zeros((tm, LANES), F32)
    for idx, val in enumerate((ex1, ex2, w1, w2, rank1, rank2)):
        meta = jnp.where(lane == idx, val, meta)
    meta_ref[...] = meta


def _router(h, w_router, *, tm=512):
    T, D = h.shape
    tm = min(tm, T)
    nt = T // tm
    w_hi = w_router.astype(BF16)
    w_lo = (w_router - w_hi.astype(F32)).astype(BF16)
    return pl.pallas_call(
        _router_kernel, grid=(nt,),
        in_specs=[pl.BlockSpec((tm, D), lambda i: (i, 0)),
                  pl.BlockSpec((D, LANES), lambda i: (0, 0)),
                  pl.BlockSpec((D, LANES), lambda i: (0, 0))],
        out_specs=[pl.BlockSpec((tm, LANES), lambda i: (i, 0)),
                   pl.BlockSpec((1, SUBLANES, LANES), lambda i: (i, 0, 0))],
        out_shape=[jax.ShapeDtypeStruct((T, LANES), F32),
                   jax.ShapeDtypeStruct((nt, SUBLANES, LANES), F32)],
        scratch_shapes=[pltpu.VMEM((1, LANES), F32)],
        name="moe_router", compiler_params=_cparams("arbitrary"))(h, w_hi, w_lo)


def _pack_bf16_pairs(x):
    m = x.shape[1] // 2
    lo = pltpu.bitcast(x[:, :m].astype(BF16).astype(F32), U32)
    hi = pltpu.bitcast(x[:, m:].astype(BF16).astype(F32), U32)
    return (lo >> 16) | hi


def _unpack_bf16_pairs(p):
    return pltpu.bitcast(p << 16, F32), pltpu.bitcast(p & jnp.uint32(0xFFFF0000), F32)


SLAB_ROWS = D_MODEL // 2 // LANES


def _matrix_to_slabs(mat):
    chunks = jnp.stack([mat[:, j * LANES:(j + 1) * LANES] for j in range(SLAB_ROWS)])
    return pltpu.einshape("stl->tsl", chunks)


def _slabs_to_matrix(slabs):
    chunks = pltpu.einshape("tsl->stl", slabs)
    return jnp.concatenate([chunks[j] for j in range(SLAB_ROWS)], axis=-1)


def _dispatch_kernel(pos_ref, pad_ref, h_ref, xs_ref, slabs, zeros, sem, zsem, *, tile, n_tok):
    i = pl.program_id(0)
    base = i * tile
    cur = i % 2

    def pad_copy(e):
        return pltpu.make_async_copy(zeros, xs_ref.at[pl.ds(pad_ref[e], MOE_TILE)], zsem)

    def tail_copy(t):
        return pltpu.make_async_copy(zeros, xs_ref.at[pl.ds(t * MOE_TILE, MOE_TILE)], zsem)

    @pl.when(i == 0)
    def _():
        zeros[...] = jnp.zeros_like(zeros)
        n_tiles = xs_ref.shape[0] // MOE_TILE
        for e in range(N_EXPERTS):
            @pl.when(pad_ref[e] >= 0)
            def _():
                pad_copy(e).start()
        lax.fori_loop(pad_ref[N_EXPERTS], n_tiles, lambda t, c: (tail_copy(t).start(), c)[1], 0)
        for e in range(N_EXPERTS):
            @pl.when(pad_ref[e] >= 0)
            def _():
                pad_copy(e).wait()
        lax.fori_loop(pad_ref[N_EXPERTS], n_tiles, lambda t, c: (tail_copy(t).wait(), c)[1], 0)

    def wait_rows(buf):
        for _ in range(2):
            pltpu.make_async_copy(slabs.at[buf], xs_ref.at[pl.ds(0, tile)], sem.at[buf]).wait()

    slabs[cur] = _matrix_to_slabs(_pack_bf16_pairs(h_ref[...]))

    def issue(r, c):
        for slot in range(2):
            pltpu.make_async_copy(slabs.at[cur, r], xs_ref.at[pos_ref[slot * n_tok + base + r]],
                                  sem.at[cur]).start()
        return c

    lax.fori_loop(0, tile, issue, 0, unroll=8)

    @pl.when(i > 0)
    def _():
        wait_rows(1 - cur)

    @pl.when(i == pl.num_programs(0) - 1)
    def _():
        wait_rows(cur)


def _dispatch(pos, pad_start, h, *, n_rows, tile=256):
    T, D = h.shape
    tile = min(tile, T)
    return pl.pallas_call(
        functools.partial(_dispatch_kernel, tile=tile, n_tok=T),
        grid_spec=pltpu.PrefetchScalarGridSpec(
            num_scalar_prefetch=2, grid=(T // tile,),
            in_specs=[pl.BlockSpec((tile, D), lambda i, pos, pad: (i, 0))],
            out_specs=pl.BlockSpec(memory_space=pl.ANY),
            scratch_shapes=[pltpu.VMEM((2, tile, SLAB_ROWS, LANES), U32),
                            pltpu.VMEM((MOE_TILE, SLAB_ROWS, LANES), U32),
                            pltpu.SemaphoreType.DMA((2,)), pltpu.SemaphoreType.DMA]),
        out_shape=jax.ShapeDtypeStruct((n_rows, SLAB_ROWS, LANES), U32),
        name="moe_dispatch", compiler_params=_cparams("arbitrary"))(pos, pad_start, h)


def _experts_kernel(te_ref, na_ref, xs_ref, wgu_ref, wd_ref, y_ref, wgu_bf, wd_bf):
    i = pl.program_id(0)

    @pl.when(i >= na_ref[0])
    def _():
        y_ref[...] = jnp.zeros_like(y_ref)

    @pl.when(i < na_ref[0])
    def _():
        @pl.when((i == 0) | (te_ref[i] != te_ref[jnp.maximum(i - 1, 0)]))
        def _():
            wgu_bf[...] = wgu_ref[0].astype(BF16)
            wd_bf[...] = wd_ref[0].astype(BF16)

        lo, hi = _unpack_bf16_pairs(_slabs_to_matrix(xs_ref[...]))
        x = jnp.concatenate([lo, hi], axis=-1).astype(BF16)
        gu = jnp.dot(x, wgu_bf[...], preferred_element_type=F32)
        hid = jax.nn.silu(gu[:, :EXPERT_FF]) * gu[:, EXPERT_FF:]
        y = jnp.dot(hid.astype(BF16), wd_bf[...], preferred_element_type=F32)
        y_ref[...] = _matrix_to_slabs(_pack_bf16_pairs(y))


def _experts(tile_expert, n_active, xs, w_gate_up, w_down):
    n_rows = xs.shape[0]
    D = D_MODEL
    tm = MOE_TILE
    row_map = lambda i, te, na: (jnp.minimum(i, na[0] - 1), 0, 0)
    w_map = lambda i, te, na: (te[jnp.minimum(i, na[0] - 1)], 0, 0)
    return pl.pallas_call(
        _experts_kernel,
        grid_spec=pltpu.PrefetchScalarGridSpec(
            num_scalar_prefetch=2, grid=(n_rows // tm,),
            in_specs=[pl.BlockSpec((tm, SLAB_ROWS, LANES), row_map),
                      pl.BlockSpec((1, D, 2 * EXPERT_FF), w_map),
                      pl.BlockSpec((1, EXPERT_FF, D), w_map)],
            out_specs=pl.BlockSpec((tm, SLAB_ROWS, LANES), lambda i, te, na: (i, 0, 0)),
            scratch_shapes=[pltpu.VMEM((D, 2 * EXPERT_FF), BF16), pltpu.VMEM((EXPERT_FF, D), BF16)]),
        out_shape=jax.ShapeDtypeStruct((n_rows, SLAB_ROWS, LANES), U32),
        name="moe_experts", compiler_params=_cparams("arbitrary"))(tile_expert, n_active, xs, w_gate_up, w_down)


def _combine_ln_kernel(pos_ref, h_ref, meta_ref, g_ref, b_ref, ys_ref, o_ref, ob_ref, slabs, sem,
                       *, tile, n_tok):
    i = pl.program_id(0)
    n_steps = pl.num_programs(0)
    cur = i % 2

    def gather_tile(step, buf):
        def issue(r, c):
            for slot in range(2):
                pltpu.make_async_copy(ys_ref.at[pos_ref[slot * n_tok + step * tile + r]],
                                      slabs.at[buf, slot, r], sem.at[buf]).start()
            return c

        lax.fori_loop(0, tile, issue, 0, unroll=8)

    @pl.when(i == 0)
    def _():
        gather_tile(0, 0)

    @pl.when(i + 1 < n_steps)
    def _():
        gather_tile(i + 1, 1 - cur)

    for slot in range(2):
        pltpu.make_async_copy(ys_ref.at[pl.ds(0, tile)], slabs.at[cur, slot], sem.at[cur]).wait()

    meta = meta_ref[...]
    halves = [None, None]
    for slot in range(2):
        lo, hi = _unpack_bf16_pairs(_slabs_to_matrix(slabs[cur, slot]))
        wgt = meta[:, 2 + slot:3 + slot]
        halves = [wgt * part if acc is None else acc + wgt * part
                  for acc, part in zip(halves, (lo, hi))]
    ffn = jnp.concatenate(halves, axis=-1)
    out = _layer_norm(ALPHA * h_ref[...] + ffn, g_ref[...], b_ref[...])
    o_ref[...] = out
    ob_ref[...] = out.astype(BF16)


def _combine_ln(pos, h, meta, gain, bias, ys, *, tile=256):
    T, D = h.shape
    tile = min(tile, T)
    tok = lambda i, pos: (i, 0)
    fixed = lambda i, pos: (0, 0)
    return pl.pallas_call(
        functools.partial(_combine_ln_kernel, tile=tile, n_tok=T),
        grid_spec=pltpu.PrefetchScalarGridSpec(
            num_scalar_prefetch=1, grid=(T // tile,),
            in_specs=[pl.BlockSpec((tile, D), tok), pl.BlockSpec((tile, LANES), tok),
                      pl.BlockSpec((1, D), fixed), pl.BlockSpec((1, D), fixed),
                      pl.BlockSpec(memory_space=pl.ANY)],
            out_specs=[pl.BlockSpec((tile, D), tok), pl.BlockSpec((tile, D), tok)],
            scratch_shapes=[pltpu.VMEM((2, 2, tile, SLAB_ROWS, LANES), U32), pltpu.SemaphoreType.DMA((2,))]),
        out_shape=[jax.ShapeDtypeStruct((T, D), F32), jax.ShapeDtypeStruct((T, D), BF16)],
        name="moe_combine_ln", compiler_params=_cparams("arbitrary"))(pos, h, meta, gain, bias, ys)


def _hierarchical_moe(h, w_router_group, w_router_expert, w_gate_up, w_down, gain, bias):
    T, D = h.shape
    w_router = jnp.concatenate(
        [w_router_group, jnp.transpose(w_router_expert, (1, 0, 2)).reshape(D, N_EXPERTS)], axis=1)
    w_router = jnp.pad(w_router.astype(F32), ((0, 0), (0, LANES - N_GROUPS - N_EXPERTS)))
    meta, counts = _router(h, w_router)

    cnt = counts[-1, 0, :N_EXPERTS].astype(jnp.int32)
    padded = (cnt + MOE_TILE - 1) // MOE_TILE * MOE_TILE
    ends = jnp.cumsum(padded)
    starts = ends - padded
    ex = meta[:, 0:2].astype(jnp.int32)
    rank = meta[:, 4:6].astype(jnp.int32)
    start_of = jnp.sum(jnp.where(ex[..., None] == jnp.arange(N_EXPERTS), starts, 0), axis=-1)
    pos = (start_of + rank).T.reshape(2 * T)
    n_rows = 2 * T + N_EXPERTS * MOE_TILE
    n_tiles = n_rows // MOE_TILE
    tile_start = jnp.arange(n_tiles, dtype=jnp.int32) * MOE_TILE
    tile_expert = jnp.minimum(jnp.sum(ends[None, :] <= tile_start[:, None], axis=1), N_EXPERTS - 1).astype(jnp.int32)
    n_active = (ends[-1:] // MOE_TILE).astype(jnp.int32)
    pad_start = jnp.concatenate([jnp.where(padded > 0, ends - MOE_TILE, -1).astype(jnp.int32), n_active])

    xs = _dispatch(pos, pad_start, h, n_rows=n_rows)
    ys = _experts(tile_expert, n_active, xs, w_gate_up, w_down)
    return _combine_ln(pos, h, meta, gain, bias, ys)


def kernel(x, ln_gain, ln_bias, gdn_w_in, gdn_conv_w, gdn_a_log, gdn_dt_bias, gdn_norm_w, gdn_w_out, fox_w_in, fox_b_forget, fox_q_norm_w, fox_k_norm_w, fox_w_out, lru_w_in, lru_conv_w, lru_conv_b, lru_w_gate_a, lru_b_gate_a, lru_w_gate_x, lru_b_gate_x, lru_a_param, lru_w_out, moe_w_router_group, moe_w_router_expert, moe_w_gate_up, moe_w_down):
    B, S, D = x.shape
    h = x.reshape(B * S, D).astype(F32)
    hb = h.astype(BF16)
    row = lambda v: v.astype(F32).reshape(1, D)
    for layer in range(DEPTH):
        kind, j = layer % 3, layer // 3
        g0, b0 = row(ln_gain[layer, 0]), row(ln_bias[layer, 0])
        g1, b1 = row(ln_gain[layer, 1]), row(ln_bias[layer, 1])
        if kind == 0:
            h = _gated_deltanet(hb, h, gdn_w_in[j], gdn_conv_w[j], gdn_a_log[j], gdn_dt_bias[j],
                                gdn_norm_w[j], gdn_w_out[j], g0, b0, B=B, S=S)
        elif kind == 1:
            h = _forgetting_attention(hb, h, fox_w_in[j], fox_b_forget[j], fox_q_norm_w[j],
                                      fox_k_norm_w[j], fox_w_out[j], g0, b0, B=B, S=S)
        else:
            h = _rglru_block(hb, h, lru_w_in[j], lru_conv_w[j], lru_conv_b[j], lru_w_gate_a[j],
                             lru_b_gate_a[j], lru_w_gate_x[j], lru_b_gate_x[j], lru_a_param[j],
                             lru_w_out[j], g0, b0, B=B, S=S)
        h, hb = _hierarchical_moe(h, moe_w_router_group[layer], moe_w_router_expert[layer],
                                  moe_w_gate_up[layer], moe_w_down[layer], g1, b1)
    return h.reshape(B, S, D).astype(x.dtype)
```

```python
import functools
import math

import jax
import jax.numpy as jnp
from jax import lax
from jax.experimental import pallas as pl
from jax.experimental.pallas import tpu as pltpu

F32 = jnp.float32
BF16 = jnp.bfloat16
U32 = jnp.uint32
HIGHEST = lax.Precision.HIGHEST

D_MODEL = 2048
N_HEADS = 16
HEAD_DIM = 128
LANES = 128
SUBLANES = 8
CONV_WIDTH = 4
GDN_CHUNK = 128
GDN_INV_BLOCK = 64
LRU_C = 8.0
N_GROUPS = 4
EXPERTS_PER_GROUP = 8
N_EXPERTS = N_GROUPS * EXPERTS_PER_GROUP
EXPERT_FF = D_MODEL // 4
DEPTH = 4
ALPHA = (2 * DEPTH) ** 0.25
LN_EPS = 1e-5
NORM_EPS = 1e-6
NEG_BIG = -1e30
LOG2E = math.log2(math.e)

MOE_TILE = 256
VMEM_LIMIT = 56 * 1024 * 1024


def _cparams(*sem):
    return pltpu.CompilerParams(dimension_semantics=sem, vmem_limit_bytes=VMEM_LIMIT)


def _bdot(a, b):
    return jnp.dot(a.astype(BF16), b.astype(BF16), preferred_element_type=F32)


def _layer_norm(y, gain, bias):
    mu = jnp.mean(y, axis=-1, keepdims=True)
    yc = y - mu
    var = jnp.mean(yc * yc, axis=-1, keepdims=True)
    return yc * lax.rsqrt(var + LN_EPS) * gain + bias


def _proj_kernel(x_ref, w_ref, *rest, epilogue, head_major):
    o_ref, w_bf = rest[-2], rest[-1]

    @pl.when(pl.program_id(1) == 0)
    def _():
        w_bf[...] = w_ref[...].astype(BF16)

    acc = jnp.dot(x_ref[...], w_bf[...], preferred_element_type=F32)
    nb = acc.shape[1] // LANES
    if epilogue == "gelu":
        acc = jax.nn.gelu(acc, approximate=True)
    for j in range(nb) if (head_major or epilogue == "rms") else ():
        blk = acc[:, j * LANES:(j + 1) * LANES]
        if epilogue == "rms":
            ms = jnp.mean(blk * blk, axis=-1, keepdims=True)
            blk = blk * lax.rsqrt(ms + NORM_EPS) * rest[0][:, j * LANES:(j + 1) * LANES]
        if head_major:
            o_ref[j] = blk.astype(o_ref.dtype)
        else:
            o_ref[:, j * LANES:(j + 1) * LANES] = blk.astype(o_ref.dtype)
    if not (head_major or epilogue == "rms"):
        o_ref[...] = acc.astype(o_ref.dtype)


def _proj(x, w, *, col0, n_cols, out_dtype, tm=1024, tn=1024, epilogue=None, row=None, head_major=False):
    T, K = x.shape
    tm, tn = min(tm, T), min(tn, n_cols)
    assert col0 % tn == 0 and n_cols % tn == 0
    in_specs = [pl.BlockSpec((tm, K), lambda j, i: (i, 0)),
                pl.BlockSpec((K, tn), lambda j, i: (0, col0 // tn + j))]
    args = [x, w]
    if row is not None:
        in_specs.append(pl.BlockSpec((1, tn), lambda j, i: (0, j)))
        args.append(row)
    if head_major:
        out_shape = jax.ShapeDtypeStruct((n_cols // LANES, T, LANES), out_dtype)
        out_spec = pl.BlockSpec((tn // LANES, tm, LANES), lambda j, i: (j, i, 0))
    else:
        out_shape = jax.ShapeDtypeStruct((T, n_cols), out_dtype)
        out_spec = pl.BlockSpec((tm, tn), lambda j, i: (i, j))
    return pl.pallas_call(
        functools.partial(_proj_kernel, epilogue=epilogue, head_major=head_major),
        grid=(n_cols // tn, T // tm), in_specs=in_specs, out_specs=out_spec, out_shape=out_shape,
        scratch_shapes=[pltpu.VMEM((K, tn), BF16)],
        name="proj" + ("_" + epilogue if epilogue else ""),
        compiler_params=_cparams("parallel", "arbitrary"))(*args)


def _outproj_ln_kernel(a_ref, w_ref, h_ref, g_ref, b_ref, o_ref):
    mix = jnp.dot(a_ref[...], w_ref[...], preferred_element_type=F32)
    o_ref[...] = _layer_norm(ALPHA * h_ref[...] + mix, g_ref[...], b_ref[...])


def _outproj_ln(a, w, h, gain, bias, *, tm=512):
    T, W = a.shape
    D = w.shape[1]
    tm = min(tm, T)
    return pl.pallas_call(
        _outproj_ln_kernel, grid=(T // tm,),
        in_specs=[pl.BlockSpec((tm, W), lambda i: (i, 0)),
                  pl.BlockSpec((W, D), lambda i: (0, 0)),
                  pl.BlockSpec((tm, D), lambda i: (i, 0)),
                  pl.BlockSpec((1, D), lambda i: (0, 0)),
                  pl.BlockSpec((1, D), lambda i: (0, 0))],
        out_specs=pl.BlockSpec((tm, D), lambda i: (i, 0)),
        out_shape=jax.ShapeDtypeStruct((T, D), F32),
        name="outproj_ln", compiler_params=_cparams("parallel"))(a, w, h, gain, bias)


def _gdn_gates_kernel(x_ref, w_ref, alog_ref, dt_ref, o_ref, *maybe_xb_ref):
    xb = x_ref[...].astype(BF16)
    for xb_ref in maybe_xb_ref:
        xb_ref[...] = xb
    acc = jnp.dot(xb, w_ref[...], preferred_element_type=F32)
    lane = lax.broadcasted_iota(jnp.int32, acc.shape, 1)
    beta = jax.nn.sigmoid(acc)
    log_decay = -jnp.exp(alog_ref[...]) * jax.nn.softplus(acc + dt_ref[...])
    o_ref[...] = jnp.where(lane < N_HEADS, beta, log_decay)


def _gdn_gates(x, w_ba, alog_row, dt_row, *, tm=1024):
    T, K = x.shape
    tm = min(tm, T)
    emit_xb = x.dtype != BF16
    tok = pl.BlockSpec((tm, K), lambda i: (i, 0))
    gate = pl.BlockSpec((tm, LANES), lambda i: (i, 0))
    out = pl.pallas_call(
        _gdn_gates_kernel, grid=(T // tm,),
        in_specs=[tok,
                  pl.BlockSpec((K, LANES), lambda i: (0, 0)),
                  pl.BlockSpec((1, LANES), lambda i: (0, 0)),
                  pl.BlockSpec((1, LANES), lambda i: (0, 0))],
        out_specs=[gate, tok] if emit_xb else gate,
        out_shape=([jax.ShapeDtypeStruct((T, LANES), F32), jax.ShapeDtypeStruct((T, K), BF16)]
                   if emit_xb else jax.ShapeDtypeStruct((T, LANES), F32)),
        name="gdn_gates", compiler_params=_cparams("parallel"))(x, w_ba, alog_row, dt_row)
    return out if emit_xb else (out, x)


def _bmm(a, b):
    return jnp.einsum("hcm,hmd->hcd", a.astype(BF16), b.astype(BF16), preferred_element_type=F32)


def _bmm_nt(a, b):
    return jnp.einsum("hcd,hmd->hcm", a.astype(BF16), b.astype(BF16), preferred_element_type=F32)


def _gdn_chunk_kernel(q_ref, k_ref, v_ref, z_ref, gt_ref, cw_ref, nw_ref, o_ref, state, ext, *, C):
    H = N_HEADS
    c = pl.program_id(1)

    @pl.when(c == 0)
    def _():
        state[...] = jnp.zeros_like(state)
        ext[:, :, 0:SUBLANES, :] = jnp.zeros((3, H, SUBLANES, LANES), F32)

    acts = []
    for i, ref in enumerate((q_ref, k_ref, v_ref)):
        raw = ref[...].astype(F32)
        ext[i, :, SUBLANES:SUBLANES + C, :] = raw
        acc = None
        for kk in range(CONV_WIDTH):
            start = SUBLANES - (CONV_WIDTH - 1) + kk
            term = ext[i, :, pl.ds(start, C), :] * cw_ref[kk, i * H:(i + 1) * H]
            acc = term if acc is None else acc + term
        ext[i, :, 0:SUBLANES, :] = raw[:, C - SUBLANES:, :]
        acts.append(acc * jax.nn.sigmoid(acc))
    qc, kc, v = acts
    q = qc * lax.rsqrt(jnp.sum(qc * qc, axis=-1, keepdims=True) + NORM_EPS) * (HEAD_DIM ** -0.5)
    k = kc * lax.rsqrt(jnp.sum(kc * kc, axis=-1, keepdims=True) + NORM_EPS)

    gt = gt_ref[...]
    ri = lax.broadcasted_iota(jnp.int32, (C, C), 0)
    ci = lax.broadcasted_iota(jnp.int32, (C, C), 1)
    causal = ri >= ci
    strict = ri > ci
    g_all = jnp.dot(causal.astype(F32), gt, precision=HIGHEST, preferred_element_type=F32)
    sel = (lax.broadcasted_iota(jnp.int32, (H, LANES), 1)
           == lax.broadcasted_iota(jnp.int32, (H, LANES), 0) + H).astype(F32)
    g_t = lax.dot_general(sel, g_all, (((1,), (1,)), ((), ())), precision=HIGHEST,
                          preferred_element_type=F32)
    g_col = jnp.stack([g_all[:, H + h:H + h + 1] for h in range(H)])
    g_row = jnp.stack([g_t[h:h + 1, :] for h in range(H)])
    beta = jnp.stack([gt[:, h:h + 1] for h in range(H)])
    decay = jnp.exp(jnp.where(causal[None], g_col - g_row, -jnp.inf))
    exp_g = jnp.exp(g_col)
    g_last = g_col[:, C - 1:C, :]

    kb = k * beta
    lower = jnp.where(strict[None], _bmm_nt(kb, k) * decay, 0.0)
    eye = (ri == ci).astype(F32)[None]
    same_block = (ri // GDN_INV_BLOCK == ci // GDN_INV_BLOCK)[None]
    m = jnp.where(same_block, -lower, 0.0)
    tinv = eye + m
    for _ in range(int(math.log2(min(C, GDN_INV_BLOCK))) - 1):
        m = _bmm(m, m)
        tinv = tinv + _bmm(tinv, m)
    if C > GDN_INV_BLOCK:
        assert C == 2 * GDN_INV_BLOCK
        off_block = jnp.where(same_block, 0.0, lower)
        tinv = tinv - _bmm(_bmm(tinv, off_block), tinv)
    rhs = jnp.concatenate([v * beta, kb * exp_g], axis=-1)
    sol = _bmm(tinv, rhs)
    u, w = sol[..., :HEAD_DIM], sol[..., HEAD_DIM:]
    attn = jnp.where(causal[None], _bmm_nt(q, k) * decay, 0.0)

    s = state[...]
    v_new = u - _bmm(w, s)
    o = _bmm(q * exp_g, s) + _bmm(attn, v_new)
    k_dec = k * jnp.exp(g_last - g_col)
    state[...] = s * jnp.exp(g_last) + jnp.einsum(
        "hcd,hce->hde", k_dec.astype(BF16), v_new.astype(BF16), preferred_element_type=F32)

    o = o * lax.rsqrt(jnp.mean(o * o, axis=-1, keepdims=True) + NORM_EPS) * nw_ref[...]
    z = z_ref[...].astype(F32)
    o = o * (z * jax.nn.sigmoid(z))
    for h in range(H):
        o_ref[:, h * LANES:(h + 1) * LANES] = o[h].astype(o_ref.dtype)


def _gdn_chunk(proj_hm, gates, conv_w, norm_w, *, B, S, C=GDN_CHUNK):
    H = N_HEADS
    T = B * S
    nc = S // C
    slab = lambda j: pl.BlockSpec((H, C, LANES), lambda b, c: (j, b * nc + c, 0))
    return pl.pallas_call(
        functools.partial(_gdn_chunk_kernel, C=C), grid=(B, nc),
        in_specs=[slab(0), slab(1), slab(2), slab(3),
                  pl.BlockSpec((C, LANES), lambda b, c: (b * nc + c, 0)),
                  pl.BlockSpec((CONV_WIDTH, 3 * H, 1, LANES), lambda b, c: (0, 0, 0, 0)),
                  pl.BlockSpec((1, 1, LANES), lambda b, c: (0, 0, 0))],
        out_specs=pl.BlockSpec((C, H * LANES), lambda b, c: (b * nc + c, 0)),
        out_shape=jax.ShapeDtypeStruct((T, H * LANES), BF16),
        scratch_shapes=[pltpu.VMEM((H, HEAD_DIM, HEAD_DIM), F32),
                        pltpu.VMEM((3, H, C + SUBLANES, LANES), F32)],
        name="gdn_chunk", compiler_params=_cparams("parallel", "arbitrary"))(
            proj_hm, proj_hm, proj_hm, proj_hm, gates, conv_w, norm_w)


def _gated_deltanet(x, h, w_in, conv_w, a_log, dt_bias, norm_w, w_out, gain, bias, *, B, S):
    W, H = N_HEADS * HEAD_DIM, N_HEADS
    w_ba = jnp.pad(w_in[:, 4 * W:], ((0, 0), (0, LANES - 2 * H))).astype(BF16)
    lane_pad = lambda v: jnp.pad(v.astype(F32), (H, LANES - 2 * H)).reshape(1, LANES)
    gates, xb = _gdn_gates(x, w_ba, lane_pad(a_log), lane_pad(dt_bias))
    proj_hm = _proj(xb, w_in, col0=0, n_cols=4 * W, out_dtype=BF16, head_major=True)
    o = _gdn_chunk(proj_hm, gates, conv_w.reshape(CONV_WIDTH, 3 * H, 1, LANES).astype(F32),
                   norm_w.reshape(1, 1, LANES).astype(F32), B=B, S=S)
    return _outproj_ln(o, w_out.astype(BF16), h, gain, bias)


def _fox_cum_kernel(x_ref, w_ref, b_ref, o_ref, carry, *, tiles_per_seq):
    i = pl.program_id(0)

    @pl.when(i % tiles_per_seq == 0)
    def _():
        carry[...] = jnp.zeros_like(carry)

    acc = jnp.dot(x_ref[...], w_ref[...], preferred_element_type=F32)
    log_f = jax.nn.log_sigmoid(acc + b_ref[...])
    tg = acc.shape[0]
    tri = (lax.broadcasted_iota(jnp.int32, (tg, tg), 0)
           >= lax.broadcasted_iota(jnp.int32, (tg, tg), 1)).astype(F32)
    cum = jnp.dot(tri, log_f, precision=HIGHEST, preferred_element_type=F32) + carry[...]
    carry[...] = cum[tg - 1:tg, :]
    o_ref[...] = cum


def _fox_cum(xb, w_f, b_row, *, S, tg=512):
    T, K = xb.shape
    tg = min(tg, S)
    return pl.pallas_call(
        functools.partial(_fox_cum_kernel, tiles_per_seq=S // tg), grid=(T // tg,),
        in_specs=[pl.BlockSpec((tg, K), lambda i: (i, 0)),
                  pl.BlockSpec((K, LANES), lambda i: (0, 0)),
                  pl.BlockSpec((1, LANES), lambda i: (0, 0))],
        out_specs=pl.BlockSpec((tg, LANES), lambda i: (i, 0)),
        out_shape=jax.ShapeDtypeStruct((T, LANES), F32),
        scratch_shapes=[pltpu.VMEM((1, LANES), F32)],
        name="fox_cum", compiler_params=_cparams("arbitrary"))(xb, w_f, b_row)


def _split3_bf16(x):
    hi = x.astype(BF16).astype(F32)
    mid = (x - hi).astype(BF16).astype(F32)
    lo = (x - hi - mid).astype(BF16).astype(F32)
    return hi, mid, lo


FOX_HEADS_PER_STEP = 4


def _fox_attn_kernel(q_ref, k_ref, v_ref, z_ref, cum_ref, o_ref, k_aug, v_aug, *, blk):
    HP = FOX_HEADS_PER_STEP
    group = pl.program_id(0) % (N_HEADS // HP)
    qi = pl.program_id(1)
    S = k_ref.shape[1]

    @pl.when(qi == 0)
    def _():
        lane_row = lax.broadcasted_iota(jnp.int32, (1, LANES), 1)
        lane = lax.broadcasted_iota(jnp.int32, (S, LANES), 1)
        for j in range(HP):
            pick = (lane_row == group * HP + j).astype(F32)
            bias = -LOG2E * jnp.sum(cum_ref[...] * pick, axis=-1, keepdims=True)
            hi, mid, lo = _split3_bf16(bias)
            k_aug[j, :, :HEAD_DIM] = k_ref[j]
            k_aug[j, :, HEAD_DIM:] = jnp.where(
                lane == 0, hi, jnp.where(lane == 1, mid, jnp.where(lane == 2, lo, 0.0))).astype(BF16)
            v_aug[j, :, :HEAD_DIM] = v_ref[j]
            v_aug[j, :, HEAD_DIM:] = jnp.where(lane == 0, 1.0, 0.0).astype(BF16)

    lane_q = lax.broadcasted_iota(jnp.int32, (blk, LANES), 1)
    ones3 = jnp.where(lane_q < 3, 1.0, 0.0).astype(BF16)
    q = [jnp.concatenate([q_ref[j], ones3], axis=-1) for j in range(HP)]

    def step(kb, carry, masked):
        start = pl.multiple_of(kb * blk, blk)
        out = []
        for j in range(HP):
            m, acc = carry[j]
            s = lax.dot_general(q[j], k_aug[j, pl.ds(start, blk), :], (((1,), (1,)), ((), ())),
                                preferred_element_type=F32)
            if masked:
                keep = (lax.broadcasted_iota(jnp.int32, (blk, blk), 0)
                        >= lax.broadcasted_iota(jnp.int32, (blk, blk), 1))
                s = jnp.where(keep, s, NEG_BIG)
            m_new = jnp.maximum(m, jnp.max(s, axis=-1, keepdims=True))
            p = jnp.exp2(s - m_new)
            acc = jnp.exp2(m - m_new) * acc + jnp.dot(p.astype(BF16), v_aug[j, pl.ds(start, blk), :],
                                                      preferred_element_type=F32)
            out.append((m_new, acc))
        return tuple(out)

    init = tuple((jnp.full((blk, 1), NEG_BIG, F32), jnp.zeros((blk, 2 * HEAD_DIM), F32)) for _ in range(HP))
    carry = lax.fori_loop(0, qi, functools.partial(step, masked=False), init)
    carry = step(qi, carry, masked=True)
    for j in range(HP):
        acc = carry[j][1]
        z = z_ref[j].astype(F32)
        o = acc[:, :HEAD_DIM] / acc[:, HEAD_DIM:HEAD_DIM + 1]
        o_ref[:, j * HEAD_DIM:(j + 1) * HEAD_DIM] = (o * jax.nn.sigmoid(z)).astype(o_ref.dtype)


def _fox_attn(qk_hm, vz_hm, cum, *, B, S, blk=512):
    H, HP = N_HEADS, FOX_HEADS_PER_STEP
    G = H // HP
    T = B * S
    blk = min(blk, S)
    nq = S // blk
    return pl.pallas_call(
        functools.partial(_fox_attn_kernel, blk=blk), grid=(B * G, nq),
        in_specs=[pl.BlockSpec((HP, blk, LANES), lambda g, qi: (g % G, (g // G) * nq + qi, 0)),
                  pl.BlockSpec((HP, S, LANES), lambda g, qi: (G + g % G, g // G, 0)),
                  pl.BlockSpec((HP, S, LANES), lambda g, qi: (g % G, g // G, 0)),
                  pl.BlockSpec((HP, blk, LANES), lambda g, qi: (G + g % G, (g // G) * nq + qi, 0)),
                  pl.BlockSpec((S, LANES), lambda g, qi: (g // G, 0))],
        out_specs=pl.BlockSpec((blk, HP * LANES), lambda g, qi: ((g // G) * nq + qi, g % G)),
        out_shape=jax.ShapeDtypeStruct((T, H * LANES), BF16),
        scratch_shapes=[pltpu.VMEM((HP, S, 2 * HEAD_DIM), BF16), pltpu.VMEM((HP, S, 2 * HEAD_DIM), BF16)],
        name="fox_attn", compiler_params=_cparams("parallel", "arbitrary"))(qk_hm, qk_hm, vz_hm, vz_hm, cum)


def _forgetting_attention(xb, h, w_in, b_forget, q_norm_w, k_norm_w, w_out, gain, bias, *, B, S):
    W, H = N_HEADS * HEAD_DIM, N_HEADS
    norm_row = jnp.concatenate([jnp.tile(q_norm_w.astype(F32) * (HEAD_DIM ** -0.5 * LOG2E), H),
                                jnp.tile(k_norm_w.astype(F32), H)]).reshape(1, 2 * W)
    qk_hm = _proj(xb, w_in, col0=0, n_cols=2 * W, out_dtype=BF16, epilogue="rms", row=norm_row,
                  head_major=True)
    vz_hm = _proj(xb, w_in, col0=2 * W, n_cols=2 * W, out_dtype=BF16, head_major=True)
    w_f = jnp.pad(w_in[:, 4 * W:], ((0, 0), (0, LANES - H))).astype(BF16)
    b_row = jnp.pad(b_forget.astype(F32), (0, LANES - H)).reshape(1, LANES)
    cum = _fox_cum(xb, w_f, b_row, S=S)
    o = _fox_attn(qk_hm, vz_hm, cum, B=B, S=S)
    return _outproj_ln(o, w_out.astype(BF16), h, gain, bias)


def _lru_kernel(u_ref, y_ref, cw_ref, cb_ref, wa_ref, ba_ref, wx_ref, bx_ref, ap_ref, o_ref,
                ext, a_s, b_s, h_s, hcar, *, ts):
    t = pl.program_id(1)
    W = u_ref.shape[1]

    @pl.when(t == 0)
    def _():
        ext[0:SUBLANES, :] = jnp.zeros((SUBLANES, W), F32)
        hcar[...] = jnp.zeros_like(hcar)

    raw = u_ref[...]
    ext[SUBLANES:SUBLANES + ts, :] = raw
    u = cb_ref[...]
    for kk in range(CONV_WIDTH):
        start = SUBLANES - (CONV_WIDTH - 1) + kk
        u = u + ext[pl.ds(start, ts), :] * cw_ref[kk:kk + 1, :]
    ext[0:SUBLANES, :] = raw[ts - SUBLANES:, :]

    ra, rx = [], []
    for n in range(W // LANES):
        ub = u[:, n * LANES:(n + 1) * LANES].astype(BF16)
        ra.append(jnp.dot(ub, wa_ref[n], preferred_element_type=F32))
        rx.append(jnp.dot(ub, wx_ref[n], preferred_element_type=F32))
    r = jax.nn.sigmoid(jnp.concatenate(ra, axis=-1) + ba_ref[...])
    gate_x = jax.nn.sigmoid(jnp.concatenate(rx, axis=-1) + bx_ref[...])
    log_a = -LRU_C * r * jax.nn.softplus(-ap_ref[...])
    a_s[...] = jnp.exp(log_a)
    b_s[...] = jnp.sqrt(1.0 - jnp.exp(2.0 * log_a)) * (gate_x * u)

    rowi = lax.broadcasted_iota(jnp.int32, (SUBLANES, W), 0)

    def sub(j, hprev):
        off = pl.multiple_of(j * SUBLANES, SUBLANES)
        aa = a_s[pl.ds(off, SUBLANES), :]
        bb = b_s[pl.ds(off, SUBLANES), :]
        for d in (1, 2, 4):
            keep = rowi >= d
            a_sh = pltpu.roll(aa, d, 0)
            b_sh = pltpu.roll(bb, d, 0)
            bb = jnp.where(keep, aa * b_sh + bb, bb)
            aa = jnp.where(keep, aa * a_sh, aa)
        hh = aa * hprev + bb
        h_s[pl.ds(off, SUBLANES), :] = hh
        return hh[SUBLANES - 1:SUBLANES, :]

    hcar[...] = lax.fori_loop(0, ts // SUBLANES, sub, hcar[...])
    o_ref[...] = (h_s[...] * y_ref[...].astype(F32)).astype(o_ref.dtype)


def _lru_scan(u_raw, y, conv_w, conv_b, w_a, b_a, w_x, b_x, a_param, *, B, S, ts=256):
    T, W = u_raw.shape
    ts = min(ts, S)
    nt = S // ts
    nb = W // LANES
    tok = lambda b, t: (b * nt + t, 0)
    fixed2 = lambda b, t: (0, 0)
    fixed3 = lambda b, t: (0, 0, 0)
    return pl.pallas_call(
        functools.partial(_lru_kernel, ts=ts), grid=(B, nt),
        in_specs=[pl.BlockSpec((ts, W), tok), pl.BlockSpec((ts, W), tok),
                  pl.BlockSpec((CONV_WIDTH, W), fixed2), pl.BlockSpec((1, W), fixed2),
                  pl.BlockSpec((nb, LANES, LANES), fixed3), pl.BlockSpec((1, W), fixed2),
                  pl.BlockSpec((nb, LANES, LANES), fixed3), pl.BlockSpec((1, W), fixed2),
                  pl.BlockSpec((1, W), fixed2)],
        out_specs=pl.BlockSpec((ts, W), tok),
        out_shape=jax.ShapeDtypeStruct((T, W), BF16),
        scratch_shapes=[pltpu.VMEM((ts + SUBLANES, W), F32), pltpu.VMEM((ts, W), F32),
                        pltpu.VMEM((ts, W), F32), pltpu.VMEM((ts, W), F32), pltpu.VMEM((1, W), F32)],
        name="lru_scan", compiler_params=_cparams("parallel", "arbitrary"))(
            u_raw, y, conv_w, conv_b, w_a, b_a, w_x, b_x, a_param)


def _rglru_block(xb, h, w_in, conv_w, conv_b, w_gate_a, b_gate_a, w_gate_x, b_gate_x, a_param, w_out,
                 gain, bias, *, B, S):
    W = D_MODEL
    row = lambda v: v.astype(F32).reshape(1, W)
    y = _proj(xb, w_in, col0=0, n_cols=W, out_dtype=BF16, epilogue="gelu")
    u_raw = _proj(xb, w_in, col0=W, n_cols=W, out_dtype=F32)
    o = _lru_scan(u_raw, y, conv_w.astype(F32), row(conv_b), w_gate_a.astype(BF16), row(b_gate_a),
                  w_gate_x.astype(BF16), row(b_gate_x), row(a_param), B=B, S=S)
    return _outproj_ln(o, w_out.astype(BF16), h, gain, bias)


def _router_kernel(h_ref, whi_ref, wlo_ref, meta_ref, cnt_ref, carry):
    i = pl.program_id(0)

    @pl.when(i == 0)
    def _():
        carry[...] = jnp.zeros_like(carry)

    h = h_ref[...]
    h_hi = h.astype(BF16)
    h_lo = (h - h_hi.astype(F32)).astype(BF16)
    logits = (jnp.dot(h_hi, whi_ref[...], preferred_element_type=F32)
              + jnp.dot(h_hi, wlo_ref[...], preferred_element_type=F32)
              + jnp.dot(h_lo, whi_ref[...], preferred_element_type=F32))
    tm = logits.shape[0]
    lane = lax.broadcasted_iota(jnp.int32, (tm, LANES), 1).astype(F32)
    first = lambda mask: jnp.min(jnp.where(mask, lane, float(LANES)), axis=-1, keepdims=True)

    is_g = lane < N_GROUPS
    gmax = jnp.max(jnp.where(is_g, logits, -jnp.inf), axis=-1, keepdims=True)
    gsum = jnp.sum(jnp.where(is_g, jnp.exp(logits - gmax), 0.0), axis=-1, keepdims=True)
    g_prob = 1.0 / gsum
    g_idx = first(is_g & (logits == gmax))
    lo = N_GROUPS + EXPERTS_PER_GROUP * g_idx
    is_e = (lane >= lo) & (lane < lo + EXPERTS_PER_GROUP)
    l1 = jnp.max(jnp.where(is_e, logits, -jnp.inf), axis=-1, keepdims=True)
    i1 = first(is_e & (logits == l1))
    is_e2 = is_e & (lane != i1)
    l2 = jnp.max(jnp.where(is_e2, logits, -jnp.inf), axis=-1, keepdims=True)
    i2 = first(is_e2 & (logits == l2))
    e2 = jnp.exp(l2 - l1)
    w1 = g_prob / (1.0 + e2)
    w2 = g_prob * e2 / (1.0 + e2)
    ex1, ex2 = i1 - N_GROUPS, i2 - N_GROUPS

    hot1 = lane == ex1
    hot2 = lane == ex2
    hot = jnp.where(hot1 | hot2, 1.0, 0.0)
    tri = (lax.broadcasted_iota(jnp.int32, (tm, tm), 0)
           > lax.broadcasted_iota(jnp.int32, (tm, tm), 1))
    prefix = jnp.dot(tri.astype(BF16), hot.astype(BF16), preferred_element_type=F32) + carry[...]
    rank1 = jnp.sum(jnp.where(hot1, prefix, 0.0), axis=-1, keepdims=True)
    rank2 = jnp.sum(jnp.where(hot2, prefix, 0.0), axis=-1, keepdims=True)
    total = carry[...] + jnp.sum(hot, axis=0, keepdims=True)
    carry[...] = total
    cnt_ref[...] = jnp.broadcast_to(total[None], cnt_ref.shape)

    meta = jnp.zeros((tm, LANES), F32)
    for idx, val in enumerate((ex1, ex2, w1, w2, rank1, rank2)):
        meta = jnp.where(lane == idx, val, meta)
    meta_ref[...] = meta


def _router(h, w_router, *, tm=512):
    T, D = h.shape
    tm = min(tm, T)
    nt = T // tm
    w_hi = w_router.astype(BF16)
    w_lo = (w_router - w_hi.astype(F32)).astype(BF16)
    return pl.pallas_call(
        _router_kernel, grid=(nt,),
        in_specs=[pl.BlockSpec((tm, D), lambda i: (i, 0)),
                  pl.BlockSpec((D, LANES), lambda i: (0, 0)),
                  pl.BlockSpec((D, LANES), lambda i: (0, 0))],
        out_specs=[pl.BlockSpec((tm, LANES), lambda i: (i, 0)),
                   pl.BlockSpec((1, SUBLANES, LANES), lambda i: (i, 0, 0))],
        out_shape=[jax.ShapeDtypeStruct((T, LANES), F32),
                   jax.ShapeDtypeStruct((nt, SUBLANES, LANES), F32)],
        scratch_shapes=[pltpu.VMEM((1, LANES), F32)],
        name="moe_router", compiler_params=_cparams("arbitrary"))(h, w_hi, w_lo)


def _pack_bf16_pairs(x):
    m = x.shape[1] // 2
    lo = pltpu.bitcast(x[:, :m].astype(BF16).astype(F32), U32)
    hi = pltpu.bitcast(x[:, m:].astype(BF16).astype(F32), U32)
    return (lo >> 16) | hi


def _unpack_bf16_pairs(p):
    return pltpu.bitcast(p << 16, F32), pltpu.bitcast(p & jnp.uint32(0xFFFF0000), F32)


SLAB_ROWS = D_MODEL // 2 // LANES


def _matrix_to_slabs(mat):
    chunks = jnp.stack([mat[:, j * LANES:(j + 1) * LANES] for j in range(SLAB_ROWS)])
    return pltpu.einshape("stl->tsl", chunks)


def _slabs_to_matrix(slabs):
    chunks = pltpu.einshape("tsl->stl", slabs)
    return jnp.concatenate([chunks[j] for j in range(SLAB_ROWS)], axis=-1)


def _dispatch_kernel(pos_ref, pad_ref, h_ref, xs_ref, slabs, zeros, sem, zsem, *, tile, n_tok):
    i = pl.program_id(0)
    base = i * tile
    cur = i % 2

    def pad_copy(e):
        return pltpu.make_async_copy(zeros, xs_ref.at[pl.ds(pad_ref[e], MOE_TILE)], zsem)

    def tail_copy(t):
        return pltpu.make_async_copy(zeros, xs_ref.at[pl.ds(t * MOE_TILE, MOE_TILE)], zsem)

    @pl.when(i == 0)
    def _():
        zeros[...] = jnp.zeros_like(zeros)
        n_tiles = xs_ref.shape[0] // MOE_TILE
        for e in range(N_EXPERTS):
            @pl.when(pad_ref[e] >= 0)
            def _():
                pad_copy(e).start()
        lax.fori_loop(pad_ref[N_EXPERTS], n_tiles, lambda t, c: (tail_copy(t).start(), c)[1], 0)
        for e in range(N_EXPERTS):
            @pl.when(pad_ref[e] >= 0)
            def _():
                pad_copy(e).wait()
        lax.fori_loop(pad_ref[N_EXPERTS], n_tiles, lambda t, c: (tail_copy(t).wait(), c)[1], 0)

    def wait_rows(buf):
        for _ in range(2):
            pltpu.make_async_copy(slabs.at[buf], xs_ref.at[pl.ds(0, tile)], sem.at[buf]).wait()

    slabs[cur] = _matrix_to_slabs(_pack_bf16_pairs(h_ref[...]))

    def issue(r, c):
        for slot in range(2):
            pltpu.make_async_copy(slabs.at[cur, r], xs_ref.at[pos_ref[slot * n_tok + base + r]],
                                  sem.at[cur]).start()
        return c

    lax.fori_loop(0, tile, issue, 0, unroll=8)

    @pl.when(i > 0)
    def _():
        wait_rows(1 - cur)

    @pl.when(i == pl.num_programs(0) - 1)
    def _():
        wait_rows(cur)


def _dispatch(pos, pad_start, h, *, n_rows, tile=256):
    T, D = h.shape
    tile = min(tile, T)
    return pl.pallas_call(
        functools.partial(_dispatch_kernel, tile=tile, n_tok=T),
        grid_spec=pltpu.PrefetchScalarGridSpec(
            num_scalar_prefetch=2, grid=(T // tile,),
            in_specs=[pl.BlockSpec((tile, D), lambda i, pos, pad: (i, 0))],
            out_specs=pl.BlockSpec(memory_space=pl.ANY),
            scratch_shapes=[pltpu.VMEM((2, tile, SLAB_ROWS, LANES), U32),
                            pltpu.VMEM((MOE_TILE, SLAB_ROWS, LANES), U32),
                            pltpu.SemaphoreType.DMA((2,)), pltpu.SemaphoreType.DMA]),
        out_shape=jax.ShapeDtypeStruct((n_rows, SLAB_ROWS, LANES), U32),
        name="moe_dispatch", compiler_params=_cparams("arbitrary"))(pos, pad_start, h)


def _experts_kernel(te_ref, na_ref, xs_ref, wgu_ref, wd_ref, y_ref, wgu_bf, wd_bf):
    i = pl.program_id(0)

    @pl.when(i >= na_ref[0])
    def _():
        y_ref[...] = jnp.zeros_like(y_ref)

    @pl.when(i < na_ref[0])
    def _():
        @pl.when((i == 0) | (te_ref[i] != te_ref[jnp.maximum(i - 1, 0)]))
        def _():
            wgu_bf[...] = wgu_ref[0].astype(BF16)
            wd_bf[...] = wd_ref[0].astype(BF16)

        lo, hi = _unpack_bf16_pairs(_slabs_to_matrix(xs_ref[...]))
        x = jnp.concatenate([lo, hi], axis=-1).astype(BF16)
        gu = jnp.dot(x, wgu_bf[...], preferred_element_type=F32)
        hid = jax.nn.silu(gu[:, :EXPERT_FF]) * gu[:, EXPERT_FF:]
        y = jnp.dot(hid.astype(BF16), wd_bf[...], preferred_element_type=F32)
        y_ref[...] = _matrix_to_slabs(_pack_bf16_pairs(y))


def _experts(tile_expert, n_active, xs, w_gate_up, w_down, layer):
    n_rows = xs.shape[0]
    D = D_MODEL
    tm = MOE_TILE
    row_map = lambda i, te, na: (jnp.minimum(i, na[0] - 1), 0, 0)
    w_map = lambda i, te, na: (layer, te[jnp.minimum(i, na[0] - 1)], 0, 0)
    return pl.pallas_call(
        _experts_kernel,
        grid_spec=pltpu.PrefetchScalarGridSpec(
            num_scalar_prefetch=2, grid=(n_rows // tm,),
            in_specs=[pl.BlockSpec((tm, SLAB_ROWS, LANES), row_map),
                      pl.BlockSpec((None, 1, D, 2 * EXPERT_FF), w_map),
                      pl.BlockSpec((None, 1, EXPERT_FF, D), w_map)],
            out_specs=pl.BlockSpec((tm, SLAB_ROWS, LANES), lambda i, te, na: (i, 0, 0)),
            scratch_shapes=[pltpu.VMEM((D, 2 * EXPERT_FF), BF16), pltpu.VMEM((EXPERT_FF, D), BF16)]),
        out_shape=jax.ShapeDtypeStruct((n_rows, SLAB_ROWS, LANES), U32),
        name="moe_experts", compiler_params=_cparams("arbitrary"))(tile_expert, n_active, xs, w_gate_up, w_down)


def _combine_ln_kernel(pos_ref, h_ref, meta_ref, g_ref, b_ref, ys_ref, o_ref, ob_ref, slabs, sem,
                       *, tile, n_tok):
    i = pl.program_id(0)
    n_steps = pl.num_programs(0)
    cur = i % 2

    def gather_tile(step, buf):
        def issue(r, c):
            for slot in range(2):
                pltpu.make_async_copy(ys_ref.at[pos_ref[slot * n_tok + step * tile + r]],
                                      slabs.at[buf, slot, r], sem.at[buf]).start()
            return c

        lax.fori_loop(0, tile, issue, 0, unroll=8)

    @pl.when(i == 0)
    def _():
        gather_tile(0, 0)

    @pl.when(i + 1 < n_steps)
    def _():
        gather_tile(i + 1, 1 - cur)

    for slot in range(2):
        pltpu.make_async_copy(ys_ref.at[pl.ds(0, tile)], slabs.at[cur, slot], sem.at[cur]).wait()

    meta = meta_ref[...]
    halves = [None, None]
    for slot in range(2):
        lo, hi = _unpack_bf16_pairs(_slabs_to_matrix(slabs[cur, slot]))
        wgt = meta[:, 2 + slot:3 + slot]
        halves = [wgt * part if acc is None else acc + wgt * part
                  for acc, part in zip(halves, (lo, hi))]
    ffn = jnp.concatenate(halves, axis=-1)
    out = _layer_norm(ALPHA * h_ref[...] + ffn, g_ref[...], b_ref[...])
    o_ref[...] = out
    ob_ref[...] = out.astype(BF16)


def _combine_ln(pos, h, meta, gain, bias, ys, *, tile=256):
    T, D = h.shape
    tile = min(tile, T)
    tok = lambda i, pos: (i, 0)
    fixed = lambda i, pos: (0, 0)
    return pl.pallas_call(
        functools.partial(_combine_ln_kernel, tile=tile, n_tok=T),
        grid_spec=pltpu.PrefetchScalarGridSpec(
            num_scalar_prefetch=1, grid=(T // tile,),
            in_specs=[pl.BlockSpec((tile, D), tok), pl.BlockSpec((tile, LANES), tok),
                      pl.BlockSpec((1, D), fixed), pl.BlockSpec((1, D), fixed),
                      pl.BlockSpec(memory_space=pl.ANY)],
            out_specs=[pl.BlockSpec((tile, D), tok), pl.BlockSpec((tile, D), tok)],
            scratch_shapes=[pltpu.VMEM((2, 2, tile, SLAB_ROWS, LANES), U32), pltpu.SemaphoreType.DMA((2,))]),
        out_shape=[jax.ShapeDtypeStruct((T, D), F32), jax.ShapeDtypeStruct((T, D), BF16)],
        name="moe_combine_ln", compiler_params=_cparams("arbitrary"))(pos, h, meta, gain, bias, ys)


def _hierarchical_moe(h, w_router_group, w_router_expert, w_gate_up, w_down, layer, gain, bias):
    T, D = h.shape
    w_router = jnp.concatenate(
        [w_router_group, jnp.transpose(w_router_expert, (1, 0, 2)).reshape(D, N_EXPERTS)], axis=1)
    w_router = jnp.pad(w_router.astype(F32), ((0, 0), (0, LANES - N_GROUPS - N_EXPERTS)))
    meta, counts = _router(h, w_router)

    cnt = counts[-1, 0, :N_EXPERTS].astype(jnp.int32)
    padded = (cnt + MOE_TILE - 1) // MOE_TILE * MOE_TILE
    ends = jnp.cumsum(padded)
    starts = ends - padded
    ex = meta[:, 0:2].astype(jnp.int32)
    rank = meta[:, 4:6].astype(jnp.int32)
    start_of = jnp.sum(jnp.where(ex[..., None] == jnp.arange(N_EXPERTS), starts, 0), axis=-1)
    pos = (start_of + rank).T.reshape(2 * T)
    n_rows = 2 * T + N_EXPERTS * MOE_TILE
    n_tiles = n_rows // MOE_TILE
    tile_start = jnp.arange(n_tiles, dtype=jnp.int32) * MOE_TILE
    tile_expert = jnp.minimum(jnp.sum(ends[None, :] <= tile_start[:, None], axis=1), N_EXPERTS - 1).astype(jnp.int32)
    n_active = (ends[-1:] // MOE_TILE).astype(jnp.int32)
    pad_start = jnp.concatenate([jnp.where(padded > 0, ends - MOE_TILE, -1).astype(jnp.int32), n_active])

    xs = _dispatch(pos, pad_start, h, n_rows=n_rows)
    ys = _experts(tile_expert, n_active, xs, w_gate_up, w_down, layer)
    return _combine_ln(pos, h, meta, gain, bias, ys)


def kernel(x, ln_gain, ln_bias, gdn_w_in, gdn_conv_w, gdn_a_log, gdn_dt_bias, gdn_norm_w, gdn_w_out, fox_w_in, fox_b_forget, fox_q_norm_w, fox_k_norm_w, fox_w_out, lru_w_in, lru_conv_w, lru_conv_b, lru_w_gate_a, lru_b_gate_a, lru_w_gate_x, lru_b_gate_x, lru_a_param, lru_w_out, moe_w_router_group, moe_w_router_expert, moe_w_gate_up, moe_w_down):
    B, S, D = x.shape
    h = x.reshape(B * S, D).astype(F32)
    hb = h
    row = lambda v: v.astype(F32).reshape(1, D)
    for layer in range(DEPTH):
        kind, j = layer % 3, layer // 3
        g0, b0 = row(ln_gain[layer, 0]), row(ln_bias[layer, 0])
        g1, b1 = row(ln_gain[layer, 1]), row(ln_bias[layer, 1])
        if kind != 0:
            hb = hb.astype(BF16)
        if kind == 0:
            h = _gated_deltanet(hb, h, gdn_w_in[j], gdn_conv_w[j], gdn_a_log[j], gdn_dt_bias[j],
                                gdn_norm_w[j], gdn_w_out[j], g0, b0, B=B, S=S)
        elif kind == 1:
            h = _forgetting_attention(hb, h, fox_w_in[j], fox_b_forget[j], fox_q_norm_w[j],
                                      fox_k_norm_w[j], fox_w_out[j], g0, b0, B=B, S=S)
        else:
            h = _rglru_block(hb, h, lru_w_in[j], lru_conv_w[j], lru_conv_b[j], lru_w_gate_a[j],
                             lru_b_gate_a[j], lru_w_gate_x[j], lru_b_gate_x[j], lru_a_param[j],
                             lru_w_out[j], g0, b0, B=B, S=S)
        h, hb = _hierarchical_moe(h, moe_w_router_group[layer], moe_w_router_expert[layer],
                                  moe_w_gate_up, moe_w_down, layer, g1, b1)
    return h.reshape(B, S, D).astype(x.dtype)
```

```python
import functools
import math

import jax
import jax.numpy as jnp
from jax import lax
from jax.experimental import pallas as pl
from jax.experimental.pallas import tpu as pltpu

F32 = jnp.float32
BF16 = jnp.bfloat16
U32 = jnp.uint32
HIGHEST = lax.Precision.HIGHEST

D_MODEL = 2048
N_HEADS = 16
HEAD_DIM = 128
LANES = 128
SUBLANES = 8
CONV_WIDTH = 4
GDN_CHUNK = 128
GDN_INV_BLOCK = 64
LRU_C = 8.0
N_GROUPS = 4
EXPERTS_PER_GROUP = 8
N_EXPERTS = N_GROUPS * EXPERTS_PER_GROUP
EXPERT_FF = D_MODEL // 4
DEPTH = 4
ALPHA = (2 * DEPTH) ** 0.25
LN_EPS = 1e-5
NORM_EPS = 1e-6
NEG_BIG = -1e30
LOG2E = math.log2(math.e)

MOE_TILE = 256
VMEM_LIMIT = 56 * 1024 * 1024


def _cparams(*sem):
    return pltpu.CompilerParams(dimension_semantics=sem, vmem_limit_bytes=VMEM_LIMIT)


def _bdot(a, b):
    return jnp.dot(a.astype(BF16), b.astype(BF16), preferred_element_type=F32)


def _layer_norm(y, gain, bias):
    mu = jnp.mean(y, axis=-1, keepdims=True)
    yc = y - mu
    var = jnp.mean(yc * yc, axis=-1, keepdims=True)
    return yc * lax.rsqrt(var + LN_EPS) * gain + bias


def _proj_kernel(x_ref, w_ref, *rest, epilogue, head_major):
    o_ref, w_bf = rest[-2], rest[-1]

    @pl.when(pl.program_id(1) == 0)
    def _():
        w_bf[...] = w_ref[...].astype(BF16)

    acc = jnp.dot(x_ref[...], w_bf[...], preferred_element_type=F32)
    nb = acc.shape[1] // LANES
    if epilogue == "gelu":
        acc = jax.nn.gelu(acc, approximate=True)
    for j in range(nb) if (head_major or epilogue == "rms") else ():
        blk = acc[:, j * LANES:(j + 1) * LANES]
        if epilogue == "rms":
            ms = jnp.mean(blk * blk, axis=-1, keepdims=True)
            blk = blk * lax.rsqrt(ms + NORM_EPS) * rest[0][:, j * LANES:(j + 1) * LANES]
        if head_major:
            o_ref[j] = blk.astype(o_ref.dtype)
        else:
            o_ref[:, j * LANES:(j + 1) * LANES] = blk.astype(o_ref.dtype)
    if not (head_major or epilogue == "rms"):
        o_ref[...] = acc.astype(o_ref.dtype)


def _proj(x, w, *, layer, col0, n_cols, out_dtype, tm=1024, tn=1024, epilogue=None, row=None,
          head_major=False):
    T, K = x.shape
    tm, tn = min(tm, T), min(tn, n_cols)
    assert col0 % tn == 0 and n_cols % tn == 0
    in_specs = [pl.BlockSpec((tm, K), lambda j, i: (i, 0)),
                pl.BlockSpec((None, K, tn), lambda j, i: (layer, 0, col0 // tn + j))]
    args = [x, w]
    if row is not None:
        in_specs.append(pl.BlockSpec((1, tn), lambda j, i: (0, j)))
        args.append(row)
    if head_major:
        out_shape = jax.ShapeDtypeStruct((n_cols // LANES, T, LANES), out_dtype)
        out_spec = pl.BlockSpec((tn // LANES, tm, LANES), lambda j, i: (j, i, 0))
    else:
        out_shape = jax.ShapeDtypeStruct((T, n_cols), out_dtype)
        out_spec = pl.BlockSpec((tm, tn), lambda j, i: (i, j))
    return pl.pallas_call(
        functools.partial(_proj_kernel, epilogue=epilogue, head_major=head_major),
        grid=(n_cols // tn, T // tm), in_specs=in_specs, out_specs=out_spec, out_shape=out_shape,
        scratch_shapes=[pltpu.VMEM((K, tn), BF16)],
        name="proj" + ("_" + epilogue if epilogue else ""),
        compiler_params=_cparams("parallel", "arbitrary"))(*args)


def _outproj_ln_kernel(a_ref, w_ref, h_ref, g_ref, b_ref, o_ref):
    mix = jnp.dot(a_ref[...], w_ref[...], preferred_element_type=F32)
    o_ref[...] = _layer_norm(ALPHA * h_ref[...] + mix, g_ref[...], b_ref[...])


def _router_kernel(h_ref, whi_ref, wlo_ref, meta_ref, cnt_ref, carry):
    @pl.when(pl.program_id(0) == 0)
    def _():
        carry[...] = jnp.zeros_like(carry)

    meta, total = _route(h_ref[...], whi_ref[...], wlo_ref[...], carry[...])
    meta_ref[...] = meta
    carry[...] = total
    cnt_ref[...] = jnp.broadcast_to(total[None], cnt_ref.shape)


def _outproj_ln_router(a, w, h, gain, bias, w_router, *, tm=512):
    T, W = a.shape
    D = w.shape[1]
    tm = min(tm, T)
    nt = T // tm
    w_hi = w_router.astype(BF16)
    w_lo = (w_router - w_hi.astype(F32)).astype(BF16)
    tok = lambda width: pl.BlockSpec((tm, width), lambda i: (i, 0))
    fixed = lambda rows, width: pl.BlockSpec((rows, width), lambda i: (0, 0))
    h_new = pl.pallas_call(
        _outproj_ln_kernel, grid=(nt,),
        in_specs=[tok(W), fixed(W, D), tok(D), fixed(1, D), fixed(1, D)],
        out_specs=tok(D), out_shape=jax.ShapeDtypeStruct((T, D), F32),
        name="outproj_ln", compiler_params=_cparams("parallel"))(a, w, h, gain, bias)
    meta, counts = pl.pallas_call(
        _router_kernel, grid=(nt,),
        in_specs=[tok(D), fixed(D, LANES), fixed(D, LANES)],
        out_specs=[tok(LANES), pl.BlockSpec((1, SUBLANES, LANES), lambda i: (i, 0, 0))],
        out_shape=[jax.ShapeDtypeStruct((T, LANES), F32), jax.ShapeDtypeStruct((nt, SUBLANES, LANES), F32)],
        scratch_shapes=[pltpu.VMEM((1, LANES), F32)],
        name="moe_router", compiler_params=_cparams("arbitrary"))(h_new, w_hi, w_lo)
    return h_new, meta, counts


def _gdn_gates_kernel(x_ref, w_ref, alog_ref, dt_ref, o_ref, *maybe_xb_ref):
    xb = x_ref[...].astype(BF16)
    for xb_ref in maybe_xb_ref:
        xb_ref[...] = xb
    acc = jnp.dot(xb, w_ref[...], preferred_element_type=F32)
    lane = lax.broadcasted_iota(jnp.int32, acc.shape, 1)
    beta = jax.nn.sigmoid(acc)
    log_decay = -jnp.exp(alog_ref[...]) * jax.nn.softplus(acc + dt_ref[...])
    o_ref[...] = jnp.where(lane < N_HEADS, beta, log_decay)


def _gdn_gates(x, w_ba, alog_row, dt_row, *, tm=1024):
    T, K = x.shape
    tm = min(tm, T)
    emit_xb = x.dtype != BF16
    tok = pl.BlockSpec((tm, K), lambda i: (i, 0))
    gate = pl.BlockSpec((tm, LANES), lambda i: (i, 0))
    out = pl.pallas_call(
        _gdn_gates_kernel, grid=(T // tm,),
        in_specs=[tok,
                  pl.BlockSpec((K, LANES), lambda i: (0, 0)),
                  pl.BlockSpec((1, LANES), lambda i: (0, 0)),
                  pl.BlockSpec((1, LANES), lambda i: (0, 0))],
        out_specs=[gate, tok] if emit_xb else gate,
        out_shape=([jax.ShapeDtypeStruct((T, LANES), F32), jax.ShapeDtypeStruct((T, K), BF16)]
                   if emit_xb else jax.ShapeDtypeStruct((T, LANES), F32)),
        name="gdn_gates", compiler_params=_cparams("parallel"))(x, w_ba, alog_row, dt_row)
    return out if emit_xb else (out, x)


def _bmm(a, b):
    return jnp.einsum("hcm,hmd->hcd", a.astype(BF16), b.astype(BF16), preferred_element_type=F32)


def _bmm_nt(a, b):
    return jnp.einsum("hcd,hmd->hcm", a.astype(BF16), b.astype(BF16), preferred_element_type=F32)


def _gdn_chunk_kernel(q_ref, k_ref, v_ref, z_ref, gt_ref, cw_ref, nw_ref, o_ref, state, ext, *, C):
    H = N_HEADS
    c = pl.program_id(1)

    @pl.when(c == 0)
    def _():
        state[...] = jnp.zeros_like(state)
        ext[:, :, 0:SUBLANES, :] = jnp.zeros((3, H, SUBLANES, LANES), F32)

    acts = []
    for i, ref in enumerate((q_ref, k_ref, v_ref)):
        raw = ref[...].astype(F32)
        ext[i, :, SUBLANES:SUBLANES + C, :] = raw
        acc = None
        for kk in range(CONV_WIDTH):
            start = SUBLANES - (CONV_WIDTH - 1) + kk
            term = ext[i, :, pl.ds(start, C), :] * cw_ref[kk, i * H:(i + 1) * H]
            acc = term if acc is None else acc + term
        ext[i, :, 0:SUBLANES, :] = raw[:, C - SUBLANES:, :]
        acts.append(acc * jax.nn.sigmoid(acc))
    qc, kc, v = acts
    q = qc * lax.rsqrt(jnp.sum(qc * qc, axis=-1, keepdims=True) + NORM_EPS) * (HEAD_DIM ** -0.5)
    k = kc * lax.rsqrt(jnp.sum(kc * kc, axis=-1, keepdims=True) + NORM_EPS)

    gt = gt_ref[...]
    ri = lax.broadcasted_iota(jnp.int32, (C, C), 0)
    ci = lax.broadcasted_iota(jnp.int32, (C, C), 1)
    causal = ri >= ci
    strict = ri > ci
    g_all = jnp.dot(causal.astype(F32), gt, precision=HIGHEST, preferred_element_type=F32)
    sel = (lax.broadcasted_iota(jnp.int32, (H, LANES), 1)
           == lax.broadcasted_iota(jnp.int32, (H, LANES), 0) + H).astype(F32)
    g_t = lax.dot_general(sel, g_all, (((1,), (1,)), ((), ())), precision=HIGHEST,
                          preferred_element_type=F32)
    g_col = jnp.stack([g_all[:, H + h:H + h + 1] for h in range(H)])
    g_row = jnp.stack([g_t[h:h + 1, :] for h in range(H)])
    beta = jnp.stack([gt[:, h:h + 1] for h in range(H)])
    decay = jnp.exp(jnp.where(causal[None], g_col - g_row, -jnp.inf))
    exp_g = jnp.exp(g_col)
    g_last = g_col[:, C - 1:C, :]

    kb = k * beta
    lower = jnp.where(strict[None], _bmm_nt(kb, k) * decay, 0.0)
    eye = (ri == ci).astype(F32)[None]
    same_block = (ri // GDN_INV_BLOCK == ci // GDN_INV_BLOCK)[None]
    m = jnp.where(same_block, -lower, 0.0)
    tinv = eye + m
    for _ in range(int(math.log2(min(C, GDN_INV_BLOCK))) - 1):
        m = _bmm(m, m)
        tinv = tinv + _bmm(tinv, m)
    if C > GDN_INV_BLOCK:
        assert C == 2 * GDN_INV_BLOCK
        off_block = jnp.where(same_block, 0.0, lower)
        tinv = tinv - _bmm(_bmm(tinv, off_block), tinv)
    rhs = jnp.concatenate([v * beta, kb * exp_g], axis=-1)
    sol = _bmm(tinv, rhs)
    u, w = sol[..., :HEAD_DIM], sol[..., HEAD_DIM:]
    attn = jnp.where(causal[None], _bmm_nt(q, k) * decay, 0.0)

    s = state[...]
    v_new = u - _bmm(w, s)
    o = _bmm(q * exp_g, s) + _bmm(attn, v_new)
    k_dec = k * jnp.exp(g_last - g_col)
    state[...] = s * jnp.exp(g_last) + jnp.einsum(
        "hcd,hce->hde", k_dec.astype(BF16), v_new.astype(BF16), preferred_element_type=F32)

    o = o * lax.rsqrt(jnp.mean(o * o, axis=-1, keepdims=True) + NORM_EPS) * nw_ref[...]
    z = z_ref[...].astype(F32)
    o = o * (z * jax.nn.sigmoid(z))
    for h in range(H):
        o_ref[:, h * LANES:(h + 1) * LANES] = o[h].astype(o_ref.dtype)


def _gdn_chunk(proj_hm, gates, conv_w, norm_w, *, B, S, C=GDN_CHUNK):
    H = N_HEADS
    T = B * S
    nc = S // C
    slab = lambda j: pl.BlockSpec((H, C, LANES), lambda b, c: (j, b * nc + c, 0))
    return pl.pallas_call(
        functools.partial(_gdn_chunk_kernel, C=C), grid=(B, nc),
        in_specs=[slab(0), slab(1), slab(2), slab(3),
                  pl.BlockSpec((C, LANES), lambda b, c: (b * nc + c, 0)),
                  pl.BlockSpec((CONV_WIDTH, 3 * H, 1, LANES), lambda b, c: (0, 0, 0, 0)),
                  pl.BlockSpec((1, 1, LANES), lambda b, c: (0, 0, 0))],
        out_specs=pl.BlockSpec((C, H * LANES), lambda b, c: (b * nc + c, 0)),
        out_shape=jax.ShapeDtypeStruct((T, H * LANES), BF16),
        scratch_shapes=[pltpu.VMEM((H, HEAD_DIM, HEAD_DIM), F32),
                        pltpu.VMEM((3, H, C + SUBLANES, LANES), F32)],
        name="gdn_chunk", compiler_params=_cparams("parallel", "arbitrary"))(
            proj_hm, proj_hm, proj_hm, proj_hm, gates, conv_w, norm_w)


def _gated_deltanet(x, h, w_in, j, conv_w, a_log, dt_bias, norm_w, w_out, gain, bias, w_router, *, B, S):
    W, H = N_HEADS * HEAD_DIM, N_HEADS
    w_ba = jnp.pad(w_in[j, :, 4 * W:], ((0, 0), (0, LANES - 2 * H))).astype(BF16)
    lane_pad = lambda v: jnp.pad(v.astype(F32), (H, LANES - 2 * H)).reshape(1, LANES)
    gates, xb = _gdn_gates(x, w_ba, lane_pad(a_log), lane_pad(dt_bias))
    proj_hm = _proj(xb, w_in, layer=j, col0=0, n_cols=4 * W, out_dtype=BF16, head_major=True)
    o = _gdn_chunk(proj_hm, gates, conv_w.reshape(CONV_WIDTH, 3 * H, 1, LANES).astype(F32),
                   norm_w.reshape(1, 1, LANES).astype(F32), B=B, S=S)
    return _outproj_ln_router(o, w_out.astype(BF16), h, gain, bias, w_router)


def _fox_cum_kernel(x_ref, w_ref, b_ref, o_ref, carry, *, tiles_per_seq):
    i = pl.program_id(0)

    @pl.when(i % tiles_per_seq == 0)
    def _():
        carry[...] = jnp.zeros_like(carry)

    acc = jnp.dot(x_ref[...], w_ref[...], preferred_element_type=F32)
    log_f = jax.nn.log_sigmoid(acc + b_ref[...])
    tg = acc.shape[0]
    tri = (lax.broadcasted_iota(jnp.int32, (tg, tg), 0)
           >= lax.broadcasted_iota(jnp.int32, (tg, tg), 1)).astype(F32)
    cum = jnp.dot(tri, log_f, precision=HIGHEST, preferred_element_type=F32) + carry[...]
    carry[...] = cum[tg - 1:tg, :]
    o_ref[...] = cum


def _fox_cum(xb, w_f, b_row, *, S, tg=512):
    T, K = xb.shape
    tg = min(tg, S)
    return pl.pallas_call(
        functools.partial(_fox_cum_kernel, tiles_per_seq=S // tg), grid=(T // tg,),
        in_specs=[pl.BlockSpec((tg, K), lambda i: (i, 0)),
                  pl.BlockSpec((K, LANES), lambda i: (0, 0)),
                  pl.BlockSpec((1, LANES), lambda i: (0, 0))],
        out_specs=pl.BlockSpec((tg, LANES), lambda i: (i, 0)),
        out_shape=jax.ShapeDtypeStruct((T, LANES), F32),
        scratch_shapes=[pltpu.VMEM((1, LANES), F32)],
        name="fox_cum", compiler_params=_cparams("arbitrary"))(xb, w_f, b_row)


def _split3_bf16(x):
    hi = x.astype(BF16).astype(F32)
    mid = (x - hi).astype(BF16).astype(F32)
    lo = (x - hi - mid).astype(BF16).astype(F32)
    return hi, mid, lo


FOX_HEADS_PER_STEP = 4


def _fox_attn_kernel(q_ref, k_ref, v_ref, z_ref, cum_ref, o_ref, k_aug, v_aug, *, blk):
    HP = FOX_HEADS_PER_STEP
    group = pl.program_id(0) % (N_HEADS // HP)
    qi = pl.program_id(1)
    S = k_ref.shape[1]

    @pl.when(qi == 0)
    def _():
        lane_row = lax.broadcasted_iota(jnp.int32, (1, LANES), 1)
        lane = lax.broadcasted_iota(jnp.int32, (S, LANES), 1)
        for j in range(HP):
            pick = (lane_row == group * HP + j).astype(F32)
            bias = -LOG2E * jnp.sum(cum_ref[...] * pick, axis=-1, keepdims=True)
            hi, mid, lo = _split3_bf16(bias)
            k_aug[j, :, :HEAD_DIM] = k_ref[j]
            k_aug[j, :, HEAD_DIM:] = jnp.where(
                lane == 0, hi, jnp.where(lane == 1, mid, jnp.where(lane == 2, lo, 0.0))).astype(BF16)
            v_aug[j, :, :HEAD_DIM] = v_ref[j]
            v_aug[j, :, HEAD_DIM:] = jnp.where(lane == 0, 1.0, 0.0).astype(BF16)

    lane_q = lax.broadcasted_iota(jnp.int32, (blk, LANES), 1)
    ones3 = jnp.where(lane_q < 3, 1.0, 0.0).astype(BF16)
    q = [jnp.concatenate([q_ref[j], ones3], axis=-1) for j in range(HP)]

    def step(kb, carry, masked):
        start = pl.multiple_of(kb * blk, blk)
        out = []
        for j in range(HP):
            m, acc = carry[j]
            s = lax.dot_general(q[j], k_aug[j, pl.ds(start, blk), :], (((1,), (1,)), ((), ())),
                                preferred_element_type=F32)
            if masked:
                keep = (lax.broadcasted_iota(jnp.int32, (blk, blk), 0)
                        >= lax.broadcasted_iota(jnp.int32, (blk, blk), 1))
                s = jnp.where(keep, s, NEG_BIG)
            m_new = jnp.maximum(m, jnp.max(s, axis=-1, keepdims=True))
            p = jnp.exp2(s - m_new)
            acc = jnp.exp2(m - m_new) * acc + jnp.dot(p.astype(BF16), v_aug[j, pl.ds(start, blk), :],
                                                      preferred_element_type=F32)
            out.append((m_new, acc))
        return tuple(out)

    init = tuple((jnp.full((blk, 1), NEG_BIG, F32), jnp.zeros((blk, 2 * HEAD_DIM), F32)) for _ in range(HP))
    carry = lax.fori_loop(0, qi, functools.partial(step, masked=False), init)
    carry = step(qi, carry, masked=True)
    for j in range(HP):
        acc = carry[j][1]
        z = z_ref[j].astype(F32)
        o = acc[:, :HEAD_DIM] / acc[:, HEAD_DIM:HEAD_DIM + 1]
        o_ref[:, j * HEAD_DIM:(j + 1) * HEAD_DIM] = (o * jax.nn.sigmoid(z)).astype(o_ref.dtype)


def _fox_attn(qk_hm, vz_hm, cum, *, B, S, blk=512):
    H, HP = N_HEADS, FOX_HEADS_PER_STEP
    G = H // HP
    T = B * S
    blk = min(blk, S)
    nq = S // blk
    return pl.pallas_call(
        functools.partial(_fox_attn_kernel, blk=blk), grid=(B * G, nq),
        in_specs=[pl.BlockSpec((HP, blk, LANES), lambda g, qi: (g % G, (g // G) * nq + qi, 0)),
                  pl.BlockSpec((HP, S, LANES), lambda g, qi: (G + g % G, g // G, 0)),
                  pl.BlockSpec((HP, S, LANES), lambda g, qi: (g % G, g // G, 0)),
                  pl.BlockSpec((HP, blk, LANES), lambda g, qi: (G + g % G, (g // G) * nq + qi, 0)),
                  pl.BlockSpec((S, LANES), lambda g, qi: (g // G, 0))],
        out_specs=pl.BlockSpec((blk, HP * LANES), lambda g, qi: ((g // G) * nq + qi, g % G)),
        out_shape=jax.ShapeDtypeStruct((T, H * LANES), BF16),
        scratch_shapes=[pltpu.VMEM((HP, S, 2 * HEAD_DIM), BF16), pltpu.VMEM((HP, S, 2 * HEAD_DIM), BF16)],
        name="fox_attn", compiler_params=_cparams("parallel", "arbitrary"))(qk_hm, qk_hm, vz_hm, vz_hm, cum)


def _forgetting_attention(xb, h, w_in, j, b_forget, q_norm_w, k_norm_w, w_out, gain, bias, w_router,
                          *, B, S):
    W, H = N_HEADS * HEAD_DIM, N_HEADS
    norm_row = jnp.concatenate([jnp.tile(q_norm_w.astype(F32) * (HEAD_DIM ** -0.5 * LOG2E), H),
                                jnp.tile(k_norm_w.astype(F32), H)]).reshape(1, 2 * W)
    qk_hm = _proj(xb, w_in, layer=j, col0=0, n_cols=2 * W, out_dtype=BF16, epilogue="rms", row=norm_row,
                  head_major=True)
    vz_hm = _proj(xb, w_in, layer=j, col0=2 * W, n_cols=2 * W, out_dtype=BF16, head_major=True)
    w_f = jnp.pad(w_in[j, :, 4 * W:], ((0, 0), (0, LANES - H))).astype(BF16)
    b_row = jnp.pad(b_forget.astype(F32), (0, LANES - H)).reshape(1, LANES)
    cum = _fox_cum(xb, w_f, b_row, S=S)
    o = _fox_attn(qk_hm, vz_hm, cum, B=B, S=S)
    return _outproj_ln_router(o, w_out.astype(BF16), h, gain, bias, w_router)


def _lru_kernel(u_ref, y_ref, cw_ref, cb_ref, wa_ref, ba_ref, wx_ref, bx_ref, ap_ref, o_ref,
                ext, a_s, b_s, h_s, hcar, *, ts):
    t = pl.program_id(1)
    W = u_ref.shape[1]

    @pl.when(t == 0)
    def _():
        ext[0:SUBLANES, :] = jnp.zeros((SUBLANES, W), F32)
        hcar[...] = jnp.zeros_like(hcar)

    raw = u_ref[...]
    ext[SUBLANES:SUBLANES + ts, :] = raw
    u = cb_ref[...]
    for kk in range(CONV_WIDTH):
        start = SUBLANES - (CONV_WIDTH - 1) + kk
        u = u + ext[pl.ds(start, ts), :] * cw_ref[kk:kk + 1, :]
    ext[0:SUBLANES, :] = raw[ts - SUBLANES:, :]

    ra, rx = [], []
    for n in range(W // LANES):
        ub = u[:, n * LANES:(n + 1) * LANES].astype(BF16)
        ra.append(jnp.dot(ub, wa_ref[n], preferred_element_type=F32))
        rx.append(jnp.dot(ub, wx_ref[n], preferred_element_type=F32))
    r = jax.nn.sigmoid(jnp.concatenate(ra, axis=-1) + ba_ref[...])
    gate_x = jax.nn.sigmoid(jnp.concatenate(rx, axis=-1) + bx_ref[...])
    log_a = -LRU_C * r * jax.nn.softplus(-ap_ref[...])
    a_s[...] = jnp.exp(log_a)
    b_s[...] = jnp.sqrt(1.0 - jnp.exp(2.0 * log_a)) * (gate_x * u)

    rowi = lax.broadcasted_iota(jnp.int32, (SUBLANES, W), 0)

    def sub(j, hprev):
        off = pl.multiple_of(j * SUBLANES, SUBLANES)
        aa = a_s[pl.ds(off, SUBLANES), :]
        bb = b_s[pl.ds(off, SUBLANES), :]
        for d in (1, 2, 4):
            keep = rowi >= d
            a_sh = pltpu.roll(aa, d, 0)
            b_sh = pltpu.roll(bb, d, 0)
            bb = jnp.where(keep, aa * b_sh + bb, bb)
            aa = jnp.where(keep, aa * a_sh, aa)
        hh = aa * hprev + bb
        h_s[pl.ds(off, SUBLANES), :] = hh
        return hh[SUBLANES - 1:SUBLANES, :]

    hcar[...] = lax.fori_loop(0, ts // SUBLANES, sub, hcar[...])
    o_ref[...] = (h_s[...] * y_ref[...].astype(F32)).astype(o_ref.dtype)


def _lru_scan(u_raw, y, conv_w, conv_b, w_a, b_a, w_x, b_x, a_param, *, B, S, ts=256):
    T, W = u_raw.shape
    ts = min(ts, S)
    nt = S // ts
    nb = W // LANES
    tok = lambda b, t: (b * nt + t, 0)
    fixed2 = lambda b, t: (0, 0)
    fixed3 = lambda b, t: (0, 0, 0)
    return pl.pallas_call(
        functools.partial(_lru_kernel, ts=ts), grid=(B, nt),
        in_specs=[pl.BlockSpec((ts, W), tok), pl.BlockSpec((ts, W), tok),
                  pl.BlockSpec((CONV_WIDTH, W), fixed2), pl.BlockSpec((1, W), fixed2),
                  pl.BlockSpec((nb, LANES, LANES), fixed3), pl.BlockSpec((1, W), fixed2),
                  pl.BlockSpec((nb, LANES, LANES), fixed3), pl.BlockSpec((1, W), fixed2),
                  pl.BlockSpec((1, W), fixed2)],
        out_specs=pl.BlockSpec((ts, W), tok),
        out_shape=jax.ShapeDtypeStruct((T, W), BF16),
        scratch_shapes=[pltpu.VMEM((ts + SUBLANES, W), F32), pltpu.VMEM((ts, W), F32),
                        pltpu.VMEM((ts, W), F32), pltpu.VMEM((ts, W), F32), pltpu.VMEM((1, W), F32)],
        name="lru_scan", compiler_params=_cparams("parallel", "arbitrary"))(
            u_raw, y, conv_w, conv_b, w_a, b_a, w_x, b_x, a_param)


def _rglru_block(xb, h, w_in, j, conv_w, conv_b, w_gate_a, b_gate_a, w_gate_x, b_gate_x, a_param, w_out,
                 gain, bias, w_router, *, B, S):
    W = D_MODEL
    row = lambda v: v.astype(F32).reshape(1, W)
    y = _proj(xb, w_in, layer=j, col0=0, n_cols=W, out_dtype=BF16, epilogue="gelu")
    u_raw = _proj(xb, w_in, layer=j, col0=W, n_cols=W, out_dtype=F32)
    o = _lru_scan(u_raw, y, conv_w.astype(F32), row(conv_b), w_gate_a.astype(BF16), row(b_gate_a),
                  w_gate_x.astype(BF16), row(b_gate_x), row(a_param), B=B, S=S)
    return _outproj_ln_router(o, w_out.astype(BF16), h, gain, bias, w_router)


def _route(h, w_hi, w_lo, counts):
    h_hi = h.astype(BF16)
    h_lo = (h - h_hi.astype(F32)).astype(BF16)
    logits = (jnp.dot(h_hi, w_hi, preferred_element_type=F32)
              + jnp.dot(h_hi, w_lo, preferred_element_type=F32)
              + jnp.dot(h_lo, w_hi, preferred_element_type=F32))
    tm = logits.shape[0]
    lane = lax.broadcasted_iota(jnp.int32, (tm, LANES), 1).astype(F32)
    first = lambda mask: jnp.min(jnp.where(mask, lane, float(LANES)), axis=-1, keepdims=True)

    is_g = lane < N_GROUPS
    gmax = jnp.max(jnp.where(is_g, logits, -jnp.inf), axis=-1, keepdims=True)
    gsum = jnp.sum(jnp.where(is_g, jnp.exp(logits - gmax), 0.0), axis=-1, keepdims=True)
    g_prob = 1.0 / gsum
    g_idx = first(is_g & (logits == gmax))
    lo = N_GROUPS + EXPERTS_PER_GROUP * g_idx
    is_e = (lane >= lo) & (lane < lo + EXPERTS_PER_GROUP)
    l1 = jnp.max(jnp.where(is_e, logits, -jnp.inf), axis=-1, keepdims=True)
    i1 = first(is_e & (logits == l1))
    is_e2 = is_e & (lane != i1)
    l2 = jnp.max(jnp.where(is_e2, logits, -jnp.inf), axis=-1, keepdims=True)
    i2 = first(is_e2 & (logits == l2))
    e2 = jnp.exp(l2 - l1)
    w1 = g_prob / (1.0 + e2)
    w2 = g_prob * e2 / (1.0 + e2)
    ex1, ex2 = i1 - N_GROUPS, i2 - N_GROUPS

    hot1 = lane == ex1
    hot2 = lane == ex2
    hot = jnp.where(hot1 | hot2, 1.0, 0.0)
    tri = (lax.broadcasted_iota(jnp.int32, (tm, tm), 0)
           > lax.broadcasted_iota(jnp.int32, (tm, tm), 1))
    prefix = jnp.dot(tri.astype(BF16), hot.astype(BF16), preferred_element_type=F32) + counts
    rank1 = jnp.sum(jnp.where(hot1, prefix, 0.0), axis=-1, keepdims=True)
    rank2 = jnp.sum(jnp.where(hot2, prefix, 0.0), axis=-1, keepdims=True)
    total = counts + jnp.sum(hot, axis=0, keepdims=True)

    meta = jnp.zeros((tm, LANES), F32)
    for idx, val in enumerate((ex1, ex2, w1, w2, rank1, rank2)):
        meta = jnp.where(lane == idx, val, meta)
    return meta, total


def _router_weights(w_router_group, w_router_expert):
    D = w_router_group.shape[0]
    w = jnp.concatenate(
        [w_router_group, jnp.transpose(w_router_expert, (1, 0, 2)).reshape(D, N_EXPERTS)], axis=1)
    return jnp.pad(w.astype(F32), ((0, 0), (0, LANES - N_GROUPS - N_EXPERTS)))


def _pack_bf16_pairs(x):
    m = x.shape[1] // 2
    lo = pltpu.bitcast(x[:, :m].astype(BF16).astype(F32), U32)
    hi = pltpu.bitcast(x[:, m:].astype(BF16).astype(F32), U32)
    return (lo >> 16) | hi


def _unpack_bf16_pairs(p):
    return pltpu.bitcast(p << 16, F32), pltpu.bitcast(p & jnp.uint32(0xFFFF0000), F32)


SLAB_ROWS = D_MODEL // 2 // LANES


def _matrix_to_slabs(mat):
    chunks = jnp.stack([mat[:, j * LANES:(j + 1) * LANES] for j in range(SLAB_ROWS)])
    return pltpu.einshape("stl->tsl", chunks)


def _slabs_to_matrix(slabs):
    chunks = pltpu.einshape("tsl->stl", slabs)
    return jnp.concatenate([chunks[j] for j in range(SLAB_ROWS)], axis=-1)


def _dispatch_kernel(pos_ref, pad_ref, h_ref, xs_ref, slabs, zeros, sem, zsem, *, tile, n_tok):
    i = pl.program_id(0)
    base = i * tile
    cur = i % 2

    def pad_copy(e):
        return pltpu.make_async_copy(zeros, xs_ref.at[pl.ds(pad_ref[e], MOE_TILE)], zsem)

    def tail_copy(t):
        return pltpu.make_async_copy(zeros, xs_ref.at[pl.ds(t * MOE_TILE, MOE_TILE)], zsem)

    @pl.when(i == 0)
    def _():
        zeros[...] = jnp.zeros_like(zeros)
        n_tiles = xs_ref.shape[0] // MOE_TILE
        for e in range(N_EXPERTS):
            @pl.when(pad_ref[e] >= 0)
            def _():
                pad_copy(e).start()
        lax.fori_loop(pad_ref[N_EXPERTS], n_tiles, lambda t, c: (tail_copy(t).start(), c)[1], 0)
        for e in range(N_EXPERTS):
            @pl.when(pad_ref[e] >= 0)
            def _():
                pad_copy(e).wait()
        lax.fori_loop(pad_ref[N_EXPERTS], n_tiles, lambda t, c: (tail_copy(t).wait(), c)[1], 0)

    def wait_rows(buf):
        for _ in range(2):
            pltpu.make_async_copy(slabs.at[buf], xs_ref.at[pl.ds(0, tile)], sem.at[buf]).wait()

    slabs[cur] = _matrix_to_slabs(_pack_bf16_pairs(h_ref[...]))

    def issue(r, c):
        for slot in range(2):
            pltpu.make_async_copy(slabs.at[cur, r], xs_ref.at[pos_ref[slot * n_tok + base + r]],
                                  sem.at[cur]).start()
        return c

    lax.fori_loop(0, tile, issue, 0, unroll=8)

    @pl.when(i > 0)
    def _():
        wait_rows(1 - cur)

    @pl.when(i == pl.num_programs(0) - 1)
    def _():
        wait_rows(cur)


def _dispatch(pos, pad_start, h, *, n_rows, tile=256):
    T, D = h.shape
    tile = min(tile, T)
    return pl.pallas_call(
        functools.partial(_dispatch_kernel, tile=tile, n_tok=T),
        grid_spec=pltpu.PrefetchScalarGridSpec(
            num_scalar_prefetch=2, grid=(T // tile,),
            in_specs=[pl.BlockSpec((tile, D), lambda i, pos, pad: (i, 0))],
            out_specs=pl.BlockSpec(memory_space=pl.ANY),
            scratch_shapes=[pltpu.VMEM((2, tile, SLAB_ROWS, LANES), U32),
                            pltpu.VMEM((MOE_TILE, SLAB_ROWS, LANES), U32),
                            pltpu.SemaphoreType.DMA((2,)), pltpu.SemaphoreType.DMA]),
        out_shape=jax.ShapeDtypeStruct((n_rows, SLAB_ROWS, LANES), U32),
        name="moe_dispatch", compiler_params=_cparams("arbitrary"))(pos, pad_start, h)


def _experts_kernel(te_ref, na_ref, xs_ref, wgu_ref, wd_ref, y_ref, wgu_bf, wd_bf):
    i = pl.program_id(0)

    @pl.when(i >= na_ref[0])
    def _():
        y_ref[...] = jnp.zeros_like(y_ref)

    @pl.when(i < na_ref[0])
    def _():
        @pl.when((i == 0) | (te_ref[i] != te_ref[jnp.maximum(i - 1, 0)]))
        def _():
            wgu_bf[...] = wgu_ref[0].astype(BF16)
            wd_bf[...] = wd_ref[0].astype(BF16)

        lo, hi = _unpack_bf16_pairs(_slabs_to_matrix(xs_ref[...]))
        x = jnp.concatenate([lo, hi], axis=-1).astype(BF16)
        gu = jnp.dot(x, wgu_bf[...], preferred_element_type=F32)
        hid = jax.nn.silu(gu[:, :EXPERT_FF]) * gu[:, EXPERT_FF:]
        y = jnp.dot(hid.astype(BF16), wd_bf[...], preferred_element_type=F32)
        y_ref[...] = _matrix_to_slabs(_pack_bf16_pairs(y))


def _experts(tile_expert, n_active, xs, w_gate_up, w_down, layer):
    n_rows = xs.shape[0]
    D = D_MODEL
    tm = MOE_TILE
    row_map = lambda i, te, na: (jnp.minimum(i, na[0] - 1), 0, 0)
    w_map = lambda i, te, na: (layer, te[jnp.minimum(i, na[0] - 1)], 0, 0)
    return pl.pallas_call(
        _experts_kernel,
        grid_spec=pltpu.PrefetchScalarGridSpec(
            num_scalar_prefetch=2, grid=(n_rows // tm,),
            in_specs=[pl.BlockSpec((tm, SLAB_ROWS, LANES), row_map),
                      pl.BlockSpec((None, 1, D, 2 * EXPERT_FF), w_map),
                      pl.BlockSpec((None, 1, EXPERT_FF, D), w_map)],
            out_specs=pl.BlockSpec((tm, SLAB_ROWS, LANES), lambda i, te, na: (i, 0, 0)),
            scratch_shapes=[pltpu.VMEM((D, 2 * EXPERT_FF), BF16), pltpu.VMEM((EXPERT_FF, D), BF16)]),
        out_shape=jax.ShapeDtypeStruct((n_rows, SLAB_ROWS, LANES), U32),
        name="moe_experts", compiler_params=_cparams("arbitrary"))(tile_expert, n_active, xs, w_gate_up, w_down)


def _combine_ln_kernel(pos_ref, h_ref, meta_ref, g_ref, b_ref, ys_ref, o_ref, ob_ref, slabs, sem,
                       *, tile, n_tok):
    i = pl.program_id(0)
    n_steps = pl.num_programs(0)
    cur = i % 2

    def gather_tile(step, buf):
        def issue(r, c):
            for slot in range(2):
                pltpu.make_async_copy(ys_ref.at[pos_ref[slot * n_tok + step * tile + r]],
                                      slabs.at[buf, slot, r], sem.at[buf]).start()
            return c

        lax.fori_loop(0, tile, issue, 0, unroll=8)

    @pl.when(i == 0)
    def _():
        gather_tile(0, 0)

    @pl.when(i + 1 < n_steps)
    def _():
        gather_tile(i + 1, 1 - cur)

    for slot in range(2):
        pltpu.make_async_copy(ys_ref.at[pl.ds(0, tile)], slabs.at[cur, slot], sem.at[cur]).wait()

    meta = meta_ref[...]
    halves = [None, None]
    for slot in range(2):
        lo, hi = _unpack_bf16_pairs(_slabs_to_matrix(slabs[cur, slot]))
        wgt = meta[:, 2 + slot:3 + slot]
        halves = [wgt * part if acc is None else acc + wgt * part
                  for acc, part in zip(halves, (lo, hi))]
    ffn = jnp.concatenate(halves, axis=-1)
    out = _layer_norm(ALPHA * h_ref[...] + ffn, g_ref[...], b_ref[...])
    o_ref[...] = out
    ob_ref[...] = out.astype(BF16)


def _combine_ln(pos, h, meta, gain, bias, ys, *, tile=256):
    T, D = h.shape
    tile = min(tile, T)
    tok = lambda i, pos: (i, 0)
    fixed = lambda i, pos: (0, 0)
    return pl.pallas_call(
        functools.partial(_combine_ln_kernel, tile=tile, n_tok=T),
        grid_spec=pltpu.PrefetchScalarGridSpec(
            num_scalar_prefetch=1, grid=(T // tile,),
            in_specs=[pl.BlockSpec((tile, D), tok), pl.BlockSpec((tile, LANES), tok),
                      pl.BlockSpec((1, D), fixed), pl.BlockSpec((1, D), fixed),
                      pl.BlockSpec(memory_space=pl.ANY)],
            out_specs=[pl.BlockSpec((tile, D), tok), pl.BlockSpec((tile, D), tok)],
            scratch_shapes=[pltpu.VMEM((2, 2, tile, SLAB_ROWS, LANES), U32), pltpu.SemaphoreType.DMA((2,))]),
        out_shape=[jax.ShapeDtypeStruct((T, D), F32), jax.ShapeDtypeStruct((T, D), BF16)],
        name="moe_combine_ln", compiler_params=_cparams("arbitrary"))(pos, h, meta, gain, bias, ys)


def _hierarchical_moe(h, meta, counts, w_gate_up, w_down, layer, gain, bias):
    T, D = h.shape

    cnt = counts[-1, 0, :N_EXPERTS].astype(jnp.int32)
    padded = (cnt + MOE_TILE - 1) // MOE_TILE * MOE_TILE
    ends = jnp.cumsum(padded)
    starts = ends - padded
    ex = meta[:, 0:2].astype(jnp.int32)
    rank = meta[:, 4:6].astype(jnp.int32)
    start_of = jnp.sum(jnp.where(ex[..., None] == jnp.arange(N_EXPERTS), starts, 0), axis=-1)
    pos = (start_of + rank).T.reshape(2 * T)
    n_rows = 2 * T + N_EXPERTS * MOE_TILE
    n_tiles = n_rows // MOE_TILE
    tile_start = jnp.arange(n_tiles, dtype=jnp.int32) * MOE_TILE
    tile_expert = jnp.minimum(jnp.sum(ends[None, :] <= tile_start[:, None], axis=1), N_EXPERTS - 1).astype(jnp.int32)
    n_active = (ends[-1:] // MOE_TILE).astype(jnp.int32)
    pad_start = jnp.concatenate([jnp.where(padded > 0, ends - MOE_TILE, -1).astype(jnp.int32), n_active])

    xs = _dispatch(pos, pad_start, h, n_rows=n_rows)
    ys = _experts(tile_expert, n_active, xs, w_gate_up, w_down, layer)
    return _combine_ln(pos, h, meta, gain, bias, ys)


def kernel(x, ln_gain, ln_bias, gdn_w_in, gdn_conv_w, gdn_a_log, gdn_dt_bias, gdn_norm_w, gdn_w_out, fox_w_in, fox_b_forget, fox_q_norm_w, fox_k_norm_w, fox_w_out, lru_w_in, lru_conv_w, lru_conv_b, lru_w_gate_a, lru_b_gate_a, lru_w_gate_x, lru_b_gate_x, lru_a_param, lru_w_out, moe_w_router_group, moe_w_router_expert, moe_w_gate_up, moe_w_down):
    B, S, D = x.shape
    h = x.reshape(B * S, D).astype(F32)
    hb = h
    row = lambda v: v.astype(F32).reshape(1, D)
    for layer in range(DEPTH):
        kind, j = layer % 3, layer // 3
        g0, b0 = row(ln_gain[layer, 0]), row(ln_bias[layer, 0])
        g1, b1 = row(ln_gain[layer, 1]), row(ln_bias[layer, 1])
        w_router = _router_weights(moe_w_router_group[layer], moe_w_router_expert[layer])
        if kind != 0:
            hb = hb.astype(BF16)
        if kind == 0:
            h, meta, counts = _gated_deltanet(
                hb, h, gdn_w_in, j, gdn_conv_w[j], gdn_a_log[j], gdn_dt_bias[j], gdn_norm_w[j],
                gdn_w_out[j], g0, b0, w_router, B=B, S=S)
        elif kind == 1:
            h, meta, counts = _forgetting_attention(
                hb, h, fox_w_in, j, fox_b_forget[j], fox_q_norm_w[j], fox_k_norm_w[j], fox_w_out[j],
                g0, b0, w_router, B=B, S=S)
        else:
            h, meta, counts = _rglru_block(
                hb, h, lru_w_in, j, lru_conv_w[j], lru_conv_b[j], lru_w_gate_a[j], lru_b_gate_a[j],
                lru_w_gate_x[j], lru_b_gate_x[j], lru_a_param[j], lru_w_out[j], g0, b0, w_router, B=B, S=S)
        h, hb = _hierarchical_moe(h, meta, counts, moe_w_gate_up, moe_w_down, layer, g1, b1)
    return h.reshape(B, S, D).astype(x.dtype)
```

```python
import functools
import math

import jax
import jax.numpy as jnp
from jax import lax
from jax.experimental import pallas as pl
from jax.experimental.pallas import tpu as pltpu

F32 = jnp.float32
BF16 = jnp.bfloat16
U32 = jnp.uint32
HIGHEST = lax.Precision.HIGHEST

D_MODEL = 2048
N_HEADS = 16
HEAD_DIM = 128
LANES = 128
SUBLANES = 8
CONV_WIDTH = 4
GDN_CHUNK = 128
GDN_INV_BLOCK = 64
LRU_C = 8.0
N_GROUPS = 4
EXPERTS_PER_GROUP = 8
N_EXPERTS = N_GROUPS * EXPERTS_PER_GROUP
EXPERT_FF = D_MODEL // 4
DEPTH = 4
ALPHA = (2 * DEPTH) ** 0.25
LN_EPS = 1e-5
NORM_EPS = 1e-6
NEG_BIG = -1e30
LOG2E = math.log2(math.e)

MOE_TILE = 256
VMEM_LIMIT = 56 * 1024 * 1024


def _cparams(*sem):
    return pltpu.CompilerParams(dimension_semantics=sem, vmem_limit_bytes=VMEM_LIMIT)


def _bdot(a, b):
    return jnp.dot(a.astype(BF16), b.astype(BF16), preferred_element_type=F32)


def _layer_norm(y, gain, bias):
    mu = jnp.mean(y, axis=-1, keepdims=True)
    yc = y - mu
    var = jnp.mean(yc * yc, axis=-1, keepdims=True)
    return yc * lax.rsqrt(var + LN_EPS) * gain + bias


def _proj_kernel(x_ref, w_ref, *rest, epilogue, head_major):
    o_ref, w_bf = rest[-2], rest[-1]

    @pl.when(pl.program_id(1) == 0)
    def _():
        w_bf[...] = w_ref[...].astype(BF16)

    acc = jnp.dot(x_ref[...], w_bf[...], preferred_element_type=F32)
    nb = acc.shape[1] // LANES
    if epilogue == "gelu":
        acc = jax.nn.gelu(acc, approximate=True)
    for j in range(nb) if (head_major or epilogue == "rms") else ():
        blk = acc[:, j * LANES:(j + 1) * LANES]
        if epilogue == "rms":
            ms = jnp.mean(blk * blk, axis=-1, keepdims=True)
            blk = blk * lax.rsqrt(ms + NORM_EPS) * rest[0][:, j * LANES:(j + 1) * LANES]
        if head_major:
            o_ref[j] = blk.astype(o_ref.dtype)
        else:
            o_ref[:, j * LANES:(j + 1) * LANES] = blk.astype(o_ref.dtype)
    if not (head_major or epilogue == "rms"):
        o_ref[...] = acc.astype(o_ref.dtype)


def _proj(x, w, *, layer, col0, n_cols, out_dtype, tm=1024, tn=1024, epilogue=None, row=None,
          head_major=False):
    T, K = x.shape
    tm, tn = min(tm, T), min(tn, n_cols)
    assert col0 % tn == 0 and n_cols % tn == 0
    in_specs = [pl.BlockSpec((tm, K), lambda j, i: (i, 0)),
                pl.BlockSpec((None, K, tn), lambda j, i: (layer, 0, col0 // tn + j))]
    args = [x, w]
    if row is not None:
        in_specs.append(pl.BlockSpec((1, tn), lambda j, i: (0, j)))
        args.append(row)
    if head_major:
        out_shape = jax.ShapeDtypeStruct((n_cols // LANES, T, LANES), out_dtype)
        out_spec = pl.BlockSpec((tn // LANES, tm, LANES), lambda j, i: (j, i, 0))
    else:
        out_shape = jax.ShapeDtypeStruct((T, n_cols), out_dtype)
        out_spec = pl.BlockSpec((tm, tn), lambda j, i: (i, j))
    return pl.pallas_call(
        functools.partial(_proj_kernel, epilogue=epilogue, head_major=head_major),
        grid=(n_cols // tn, T // tm), in_specs=in_specs, out_specs=out_spec, out_shape=out_shape,
        scratch_shapes=[pltpu.VMEM((K, tn), BF16)],
        name="proj" + ("_" + epilogue if epilogue else ""),
        compiler_params=_cparams("parallel", "arbitrary"))(*args)


def _outproj_ln_kernel(a_ref, w_ref, h_ref, g_ref, b_ref, o_ref):
    mix = jnp.dot(a_ref[...], w_ref[...], preferred_element_type=F32)
    o_ref[...] = _layer_norm(ALPHA * h_ref[...] + mix, g_ref[...], b_ref[...])


def _router_kernel(h_ref, whi_ref, wlo_ref, meta_ref, cnt_ref, carry):
    @pl.when(pl.program_id(0) == 0)
    def _():
        carry[...] = jnp.zeros_like(carry)

    meta, total = _route(h_ref[...], whi_ref[...], wlo_ref[...], carry[...])
    meta_ref[...] = meta
    carry[...] = total
    cnt_ref[...] = jnp.broadcast_to(total[None], cnt_ref.shape)


def _outproj_ln_router(a, w, h, gain, bias, w_router, *, tm=512):
    T, W = a.shape
    D = w.shape[1]
    tm = min(tm, T)
    nt = T // tm
    w_hi = w_router.astype(BF16)
    w_lo = (w_router - w_hi.astype(F32)).astype(BF16)
    tok = lambda width: pl.BlockSpec((tm, width), lambda i: (i, 0))
    fixed = lambda rows, width: pl.BlockSpec((rows, width), lambda i: (0, 0))
    h_new = pl.pallas_call(
        _outproj_ln_kernel, grid=(nt,),
        in_specs=[tok(W), fixed(W, D), tok(D), fixed(1, D), fixed(1, D)],
        out_specs=tok(D), out_shape=jax.ShapeDtypeStruct((T, D), F32),
        name="outproj_ln", compiler_params=_cparams("parallel"))(a, w, h, gain, bias)
    meta, counts = pl.pallas_call(
        _router_kernel, grid=(nt,),
        in_specs=[tok(D), fixed(D, LANES), fixed(D, LANES)],
        out_specs=[tok(LANES), pl.BlockSpec((1, SUBLANES, LANES), lambda i: (i, 0, 0))],
        out_shape=[jax.ShapeDtypeStruct((T, LANES), F32), jax.ShapeDtypeStruct((nt, SUBLANES, LANES), F32)],
        scratch_shapes=[pltpu.VMEM((1, LANES), F32)],
        name="moe_router", compiler_params=_cparams("arbitrary"))(h_new, w_hi, w_lo)
    return h_new, meta, counts


def _gdn_gates_kernel(x_ref, w_ref, alog_ref, dt_ref, o_ref, *maybe_xb_ref):
    xb = x_ref[...].astype(BF16)
    for xb_ref in maybe_xb_ref:
        xb_ref[...] = xb
    acc = jnp.dot(xb, w_ref[...], preferred_element_type=F32)
    lane = lax.broadcasted_iota(jnp.int32, acc.shape, 1)
    beta = jax.nn.sigmoid(acc)
    log_decay = -jnp.exp(alog_ref[...]) * jax.nn.softplus(acc + dt_ref[...])
    o_ref[...] = jnp.where(lane < N_HEADS, beta, log_decay)


def _gdn_gates(x, w_ba, alog_row, dt_row, *, tm=1024):
    T, K = x.shape
    tm = min(tm, T)
    emit_xb = x.dtype != BF16
    tok = pl.BlockSpec((tm, K), lambda i: (i, 0))
    gate = pl.BlockSpec((tm, LANES), lambda i: (i, 0))
    out = pl.pallas_call(
        _gdn_gates_kernel, grid=(T // tm,),
        in_specs=[tok,
                  pl.BlockSpec((K, LANES), lambda i: (0, 0)),
                  pl.BlockSpec((1, LANES), lambda i: (0, 0)),
                  pl.BlockSpec((1, LANES), lambda i: (0, 0))],
        out_specs=[gate, tok] if emit_xb else gate,
        out_shape=([jax.ShapeDtypeStruct((T, LANES), F32), jax.ShapeDtypeStruct((T, K), BF16)]
                   if emit_xb else jax.ShapeDtypeStruct((T, LANES), F32)),
        name="gdn_gates", compiler_params=_cparams("parallel"))(x, w_ba, alog_row, dt_row)
    return out if emit_xb else (out, x)


def _bmm(a, b):
    return jnp.einsum("hcm,hmd->hcd", a.astype(BF16), b.astype(BF16), preferred_element_type=F32)


def _bmm_nt(a, b):
    return jnp.einsum("hcd,hmd->hcm", a.astype(BF16), b.astype(BF16), preferred_element_type=F32)


def _gdn_chunk_kernel(q_ref, k_ref, v_ref, z_ref, gt_ref, cw_ref, nw_ref, o_ref, state, ext, *, C):
    H = N_HEADS
    c = pl.program_id(1)

    @pl.when(c == 0)
    def _():
        state[...] = jnp.zeros_like(state)
        ext[:, :, 0:SUBLANES, :] = jnp.zeros((3, H, SUBLANES, LANES), F32)

    acts = []
    for i, ref in enumerate((q_ref, k_ref, v_ref)):
        raw = ref[...].astype(F32)
        ext[i, :, SUBLANES:SUBLANES + C, :] = raw
        acc = None
        for kk in range(CONV_WIDTH):
            start = SUBLANES - (CONV_WIDTH - 1) + kk
            term = ext[i, :, pl.ds(start, C), :] * cw_ref[kk, i * H:(i + 1) * H]
            acc = term if acc is None else acc + term
        ext[i, :, 0:SUBLANES, :] = raw[:, C - SUBLANES:, :]
        acts.append(acc * jax.nn.sigmoid(acc))
    qc, kc, v = acts
    q = qc * lax.rsqrt(jnp.sum(qc * qc, axis=-1, keepdims=True) + NORM_EPS) * (HEAD_DIM ** -0.5)
    k = kc * lax.rsqrt(jnp.sum(kc * kc, axis=-1, keepdims=True) + NORM_EPS)

    gt = gt_ref[...]
    ri = lax.broadcasted_iota(jnp.int32, (C, C), 0)
    ci = lax.broadcasted_iota(jnp.int32, (C, C), 1)
    causal = ri >= ci
    strict = ri > ci
    g_all = jnp.dot(causal.astype(F32), gt, precision=HIGHEST, preferred_element_type=F32)
    sel = (lax.broadcasted_iota(jnp.int32, (H, LANES), 1)
           == lax.broadcasted_iota(jnp.int32, (H, LANES), 0) + H).astype(F32)
    g_t = lax.dot_general(sel, g_all, (((1,), (1,)), ((), ())), precision=HIGHEST,
                          preferred_element_type=F32)
    g_col = jnp.stack([g_all[:, H + h:H + h + 1] for h in range(H)])
    g_row = jnp.stack([g_t[h:h + 1, :] for h in range(H)])
    beta = jnp.stack([gt[:, h:h + 1] for h in range(H)])
    decay = jnp.exp(jnp.where(causal[None], g_col - g_row, -jnp.inf))
    exp_g = jnp.exp(g_col)
    g_last = g_col[:, C - 1:C, :]

    kb = k * beta
    lower = jnp.where(strict[None], _bmm_nt(kb, k) * decay, 0.0)
    eye = (ri == ci).astype(F32)[None]
    same_block = (ri // GDN_INV_BLOCK == ci // GDN_INV_BLOCK)[None]
    m = jnp.where(same_block, -lower, 0.0)
    tinv = eye + m
    for _ in range(int(math.log2(min(C, GDN_INV_BLOCK))) - 1):
        m = _bmm(m, m)
        tinv = tinv + _bmm(tinv, m)
    if C > GDN_INV_BLOCK:
        assert C == 2 * GDN_INV_BLOCK
        off_block = jnp.where(same_block, 0.0, lower)
        tinv = tinv - _bmm(_bmm(tinv, off_block), tinv)
    rhs = jnp.concatenate([v * beta, kb * exp_g], axis=-1)
    sol = _bmm(tinv, rhs)
    u, w = sol[..., :HEAD_DIM], sol[..., HEAD_DIM:]
    attn = jnp.where(causal[None], _bmm_nt(q, k) * decay, 0.0)

    s = state[...]
    v_new = u - _bmm(w, s)
    o = _bmm(q * exp_g, s) + _bmm(attn, v_new)
    k_dec = k * jnp.exp(g_last - g_col)
    state[...] = s * jnp.exp(g_last) + jnp.einsum(
        "hcd,hce->hde", k_dec.astype(BF16), v_new.astype(BF16), preferred_element_type=F32)

    o = o * lax.rsqrt(jnp.mean(o * o, axis=-1, keepdims=True) + NORM_EPS) * nw_ref[...]
    z = z_ref[...].astype(F32)
    o = o * (z * jax.nn.sigmoid(z))
    for h in range(H):
        o_ref[:, h * LANES:(h + 1) * LANES] = o[h].astype(o_ref.dtype)


def _gdn_chunk(proj_hm, gates, conv_w, norm_w, *, B, S, C=GDN_CHUNK):
    H = N_HEADS
    T = B * S
    nc = S // C
    slab = lambda j: pl.BlockSpec((H, C, LANES), lambda b, c: (j, b * nc + c, 0))
    return pl.pallas_call(
        functools.partial(_gdn_chunk_kernel, C=C), grid=(B, nc),
        in_specs=[slab(0), slab(1), slab(2), slab(3),
                  pl.BlockSpec((C, LANES), lambda b, c: (b * nc + c, 0)),
                  pl.BlockSpec((CONV_WIDTH, 3 * H, 1, LANES), lambda b, c: (0, 0, 0, 0)),
                  pl.BlockSpec((1, 1, LANES), lambda b, c: (0, 0, 0))],
        out_specs=pl.BlockSpec((C, H * LANES), lambda b, c: (b * nc + c, 0)),
        out_shape=jax.ShapeDtypeStruct((T, H * LANES), BF16),
        scratch_shapes=[pltpu.VMEM((H, HEAD_DIM, HEAD_DIM), F32),
                        pltpu.VMEM((3, H, C + SUBLANES, LANES), F32)],
        name="gdn_chunk", compiler_params=_cparams("parallel", "arbitrary"))(
            proj_hm, proj_hm, proj_hm, proj_hm, gates, conv_w, norm_w)


def _gated_deltanet(x, h, w_in, j, conv_w, a_log, dt_bias, norm_w, w_out, gain, bias, w_router, *, B, S):
    W, H = N_HEADS * HEAD_DIM, N_HEADS
    w_ba = jnp.pad(w_in[j, :, 4 * W:], ((0, 0), (0, LANES - 2 * H))).astype(BF16)
    lane_pad = lambda v: jnp.pad(v.astype(F32), (H, LANES - 2 * H)).reshape(1, LANES)
    gates, xb = _gdn_gates(x, w_ba, lane_pad(a_log), lane_pad(dt_bias))
    proj_hm = _proj(xb, w_in, layer=j, col0=0, n_cols=4 * W, out_dtype=BF16, head_major=True)
    o = _gdn_chunk(proj_hm, gates, conv_w.reshape(CONV_WIDTH, 3 * H, 1, LANES).astype(F32),
                   norm_w.reshape(1, 1, LANES).astype(F32), B=B, S=S)
    return _outproj_ln_router(o, w_out.astype(BF16), h, gain, bias, w_router)


def _fox_cum_kernel(x_ref, w_ref, b_ref, o_ref, carry, *, tiles_per_seq):
    i = pl.program_id(0)

    @pl.when(i % tiles_per_seq == 0)
    def _():
        carry[...] = jnp.zeros_like(carry)

    acc = jnp.dot(x_ref[...], w_ref[...], preferred_element_type=F32)
    log_f = jax.nn.log_sigmoid(acc + b_ref[...])
    tg = acc.shape[0]
    tri = (lax.broadcasted_iota(jnp.int32, (tg, tg), 0)
           >= lax.broadcasted_iota(jnp.int32, (tg, tg), 1)).astype(F32)
    cum = jnp.dot(tri, log_f, precision=HIGHEST, preferred_element_type=F32) + carry[...]
    carry[...] = cum[tg - 1:tg, :]
    o_ref[...] = cum


def _fox_cum(xb, w_f, b_row, *, S, tg=512):
    T, K = xb.shape
    tg = min(tg, S)
    return pl.pallas_call(
        functools.partial(_fox_cum_kernel, tiles_per_seq=S // tg), grid=(T // tg,),
        in_specs=[pl.BlockSpec((tg, K), lambda i: (i, 0)),
                  pl.BlockSpec((K, LANES), lambda i: (0, 0)),
                  pl.BlockSpec((1, LANES), lambda i: (0, 0))],
        out_specs=pl.BlockSpec((tg, LANES), lambda i: (i, 0)),
        out_shape=jax.ShapeDtypeStruct((T, LANES), F32),
        scratch_shapes=[pltpu.VMEM((1, LANES), F32)],
        name="fox_cum", compiler_params=_cparams("arbitrary"))(xb, w_f, b_row)


def _split3_bf16(x):
    hi = x.astype(BF16).astype(F32)
    mid = (x - hi).astype(BF16).astype(F32)
    lo = (x - hi - mid).astype(BF16).astype(F32)
    return hi, mid, lo


FOX_HEADS_PER_STEP = 4


def _fox_attn_kernel(q_ref, k_ref, v_ref, z_ref, cum_ref, o_ref, k_aug, v_aug, *, blk):
    HP = FOX_HEADS_PER_STEP
    group = pl.program_id(0) % (N_HEADS // HP)
    qi = pl.program_id(1)
    S = k_ref.shape[1]

    @pl.when(qi == 0)
    def _():
        lane_row = lax.broadcasted_iota(jnp.int32, (1, LANES), 1)
        lane = lax.broadcasted_iota(jnp.int32, (S, LANES), 1)
        for j in range(HP):
            pick = (lane_row == group * HP + j).astype(F32)
            bias = -LOG2E * jnp.sum(cum_ref[...] * pick, axis=-1, keepdims=True)
            hi, mid, lo = _split3_bf16(bias)
            k_aug[j, :, :HEAD_DIM] = k_ref[j]
            k_aug[j, :, HEAD_DIM:] = jnp.where(
                lane == 0, hi, jnp.where(lane == 1, mid, jnp.where(lane == 2, lo, 0.0))).astype(BF16)
            v_aug[j, :, :HEAD_DIM] = v_ref[j]
            v_aug[j, :, HEAD_DIM:] = jnp.where(lane == 0, 1.0, 0.0).astype(BF16)

    lane_q = lax.broadcasted_iota(jnp.int32, (blk, LANES), 1)
    ones3 = jnp.where(lane_q < 3, 1.0, 0.0).astype(BF16)
    q = [jnp.concatenate([q_ref[j], ones3], axis=-1) for j in range(HP)]

    def step(kb, carry, masked):
        start = pl.multiple_of(kb * blk, blk)
        out = []
        for j in range(HP):
            m, acc = carry[j]
            s = lax.dot_general(q[j], k_aug[j, pl.ds(start, blk), :], (((1,), (1,)), ((), ())),
                                preferred_element_type=F32)
            if masked:
                keep = (lax.broadcasted_iota(jnp.int32, (blk, blk), 0)
                        >= lax.broadcasted_iota(jnp.int32, (blk, blk), 1))
                s = jnp.where(keep, s, NEG_BIG)
            m_new = jnp.maximum(m, jnp.max(s, axis=-1, keepdims=True))
            p = jnp.exp2(s - m_new)
            acc = jnp.exp2(m - m_new) * acc + jnp.dot(p.astype(BF16), v_aug[j, pl.ds(start, blk), :],
                                                      preferred_element_type=F32)
            out.append((m_new, acc))
        return tuple(out)

    init = tuple((jnp.full((blk, 1), NEG_BIG, F32), jnp.zeros((blk, 2 * HEAD_DIM), F32)) for _ in range(HP))
    def two_steps(i2, carry):
        return step(2 * i2 + 1, step(2 * i2, carry, masked=False), masked=False)

    carry = lax.fori_loop(0, qi // 2, two_steps, init)
    carry = lax.cond(qi % 2 == 1, lambda c: step(qi - 1, c, masked=False), lambda c: c, carry)
    carry = step(qi, carry, masked=True)
    for j in range(HP):
        acc = carry[j][1]
        z = z_ref[j].astype(F32)
        o = acc[:, :HEAD_DIM] / acc[:, HEAD_DIM:HEAD_DIM + 1]
        o_ref[:, j * HEAD_DIM:(j + 1) * HEAD_DIM] = (o * jax.nn.sigmoid(z)).astype(o_ref.dtype)


def _fox_attn(qk_hm, vz_hm, cum, *, B, S, blk=512):
    H, HP = N_HEADS, FOX_HEADS_PER_STEP
    G = H // HP
    T = B * S
    blk = min(blk, S)
    nq = S // blk
    return pl.pallas_call(
        functools.partial(_fox_attn_kernel, blk=blk), grid=(B * G, nq),
        in_specs=[pl.BlockSpec((HP, blk, LANES), lambda g, qi: (g % G, (g // G) * nq + qi, 0)),
                  pl.BlockSpec((HP, S, LANES), lambda g, qi: (G + g % G, g // G, 0)),
                  pl.BlockSpec((HP, S, LANES), lambda g, qi: (g % G, g // G, 0)),
                  pl.BlockSpec((HP, blk, LANES), lambda g, qi: (G + g % G, (g // G) * nq + qi, 0)),
                  pl.BlockSpec((S, LANES), lambda g, qi: (g // G, 0))],
        out_specs=pl.BlockSpec((blk, HP * LANES), lambda g, qi: ((g // G) * nq + qi, g % G)),
        out_shape=jax.ShapeDtypeStruct((T, H * LANES), BF16),
        scratch_shapes=[pltpu.VMEM((HP, S, 2 * HEAD_DIM), BF16), pltpu.VMEM((HP, S, 2 * HEAD_DIM), BF16)],
        name="fox_attn", compiler_params=_cparams("parallel", "arbitrary"))(qk_hm, qk_hm, vz_hm, vz_hm, cum)


def _forgetting_attention(xb, h, w_in, j, b_forget, q_norm_w, k_norm_w, w_out, gain, bias, w_router,
                          *, B, S):
    W, H = N_HEADS * HEAD_DIM, N_HEADS
    norm_row = jnp.concatenate([jnp.tile(q_norm_w.astype(F32) * (HEAD_DIM ** -0.5 * LOG2E), H),
                                jnp.tile(k_norm_w.astype(F32), H)]).reshape(1, 2 * W)
    qk_hm = _proj(xb, w_in, layer=j, col0=0, n_cols=2 * W, out_dtype=BF16, epilogue="rms", row=norm_row,
                  head_major=True)
    vz_hm = _proj(xb, w_in, layer=j, col0=2 * W, n_cols=2 * W, out_dtype=BF16, head_major=True)
    w_f = jnp.pad(w_in[j, :, 4 * W:], ((0, 0), (0, LANES - H))).astype(BF16)
    b_row = jnp.pad(b_forget.astype(F32), (0, LANES - H)).reshape(1, LANES)
    cum = _fox_cum(xb, w_f, b_row, S=S)
    o = _fox_attn(qk_hm, vz_hm, cum, B=B, S=S)
    return _outproj_ln_router(o, w_out.astype(BF16), h, gain, bias, w_router)


def _lru_kernel(u_ref, y_ref, cw_ref, cb_ref, wa_ref, ba_ref, wx_ref, bx_ref, ap_ref, o_ref,
                ext, a_s, b_s, h_s, hcar, *, ts):
    t = pl.program_id(1)
    W = u_ref.shape[1]

    @pl.when(t == 0)
    def _():
        ext[0:SUBLANES, :] = jnp.zeros((SUBLANES, W), F32)
        hcar[...] = jnp.zeros_like(hcar)

    raw = u_ref[...]
    ext[SUBLANES:SUBLANES + ts, :] = raw
    u = cb_ref[...]
    for kk in range(CONV_WIDTH):
        start = SUBLANES - (CONV_WIDTH - 1) + kk
        u = u + ext[pl.ds(start, ts), :] * cw_ref[kk:kk + 1, :]
    ext[0:SUBLANES, :] = raw[ts - SUBLANES:, :]

    ra, rx = [], []
    for n in range(W // LANES):
        ub = u[:, n * LANES:(n + 1) * LANES].astype(BF16)
        ra.append(jnp.dot(ub, wa_ref[n], preferred_element_type=F32))
        rx.append(jnp.dot(ub, wx_ref[n], preferred_element_type=F32))
    r = jax.nn.sigmoid(jnp.concatenate(ra, axis=-1) + ba_ref[...])
    gate_x = jax.nn.sigmoid(jnp.concatenate(rx, axis=-1) + bx_ref[...])
    log_a = -LRU_C * r * jax.nn.softplus(-ap_ref[...])
    a_s[...] = jnp.exp(log_a)
    b_s[...] = jnp.sqrt(1.0 - jnp.exp(2.0 * log_a)) * (gate_x * u)

    rowi = lax.broadcasted_iota(jnp.int32, (SUBLANES, W), 0)

    def sub(j, hprev):
        off = pl.multiple_of(j * SUBLANES, SUBLANES)
        aa = a_s[pl.ds(off, SUBLANES), :]
        bb = b_s[pl.ds(off, SUBLANES), :]
        for d in (1, 2, 4):
            keep = rowi >= d
            a_sh = pltpu.roll(aa, d, 0)
            b_sh = pltpu.roll(bb, d, 0)
            bb = jnp.where(keep, aa * b_sh + bb, bb)
            aa = jnp.where(keep, aa * a_sh, aa)
        hh = aa * hprev + bb
        h_s[pl.ds(off, SUBLANES), :] = hh
        return hh[SUBLANES - 1:SUBLANES, :]

    hcar[...] = lax.fori_loop(0, ts // SUBLANES, sub, hcar[...])
    o_ref[...] = (h_s[...] * y_ref[...].astype(F32)).astype(o_ref.dtype)


def _lru_scan(u_raw, y, conv_w, conv_b, w_a, b_a, w_x, b_x, a_param, *, B, S, ts=256):
    T, W = u_raw.shape
    ts = min(ts, S)
    nt = S // ts
    nb = W // LANES
    tok = lambda b, t: (b * nt + t, 0)
    fixed2 = lambda b, t: (0, 0)
    fixed3 = lambda b, t: (0, 0, 0)
    return pl.pallas_call(
        functools.partial(_lru_kernel, ts=ts), grid=(B, nt),
        in_specs=[pl.BlockSpec((ts, W), tok), pl.BlockSpec((ts, W), tok),
                  pl.BlockSpec((CONV_WIDTH, W), fixed2), pl.BlockSpec((1, W), fixed2),
                  pl.BlockSpec((nb, LANES, LANES), fixed3), pl.BlockSpec((1, W), fixed2),
                  pl.BlockSpec((nb, LANES, LANES), fixed3), pl.BlockSpec((1, W), fixed2),
                  pl.BlockSpec((1, W), fixed2)],
        out_specs=pl.BlockSpec((ts, W), tok),
        out_shape=jax.ShapeDtypeStruct((T, W), BF16),
        scratch_shapes=[pltpu.VMEM((ts + SUBLANES, W), F32), pltpu.VMEM((ts, W), F32),
                        pltpu.VMEM((ts, W), F32), pltpu.VMEM((ts, W), F32), pltpu.VMEM((1, W), F32)],
        name="lru_scan", compiler_params=_cparams("parallel", "arbitrary"))(
            u_raw, y, conv_w, conv_b, w_a, b_a, w_x, b_x, a_param)


def _rglru_block(xb, h, w_in, j, conv_w, conv_b, w_gate_a, b_gate_a, w_gate_x, b_gate_x, a_param, w_out,
                 gain, bias, w_router, *, B, S):
    W = D_MODEL
    row = lambda v: v.astype(F32).reshape(1, W)
    y = _proj(xb, w_in, layer=j, col0=0, n_cols=W, out_dtype=BF16, epilogue="gelu")
    u_raw = _proj(xb, w_in, layer=j, col0=W, n_cols=W, out_dtype=F32)
    o = _lru_scan(u_raw, y, conv_w.astype(F32), row(conv_b), w_gate_a.astype(BF16), row(b_gate_a),
                  w_gate_x.astype(BF16), row(b_gate_x), row(a_param), B=B, S=S)
    return _outproj_ln_router(o, w_out.astype(BF16), h, gain, bias, w_router)


def _route(h, w_hi, w_lo, counts):
    h_hi = h.astype(BF16)
    h_lo = (h - h_hi.astype(F32)).astype(BF16)
    logits = (jnp.dot(h_hi, w_hi, preferred_element_type=F32)
              + jnp.dot(h_hi, w_lo, preferred_element_type=F32)
              + jnp.dot(h_lo, w_hi, preferred_element_type=F32))
    tm = logits.shape[0]
    lane = lax.broadcasted_iota(jnp.int32, (tm, LANES), 1).astype(F32)
    first = lambda mask: jnp.min(jnp.where(mask, lane, float(LANES)), axis=-1, keepdims=True)

    is_g = lane < N_GROUPS
    gmax = jnp.max(jnp.where(is_g, logits, -jnp.inf), axis=-1, keepdims=True)
    gsum = jnp.sum(jnp.where(is_g, jnp.exp(logits - gmax), 0.0), axis=-1, keepdims=True)
    g_prob = 1.0 / gsum
    g_idx = first(is_g & (logits == gmax))
    lo = N_GROUPS + EXPERTS_PER_GROUP * g_idx
    is_e = (lane >= lo) & (lane < lo + EXPERTS_PER_GROUP)
    l1 = jnp.max(jnp.where(is_e, logits, -jnp.inf), axis=-1, keepdims=True)
    i1 = first(is_e & (logits == l1))
    is_e2 = is_e & (lane != i1)
    l2 = jnp.max(jnp.where(is_e2, logits, -jnp.inf), axis=-1, keepdims=True)
    i2 = first(is_e2 & (logits == l2))
    e2 = jnp.exp(l2 - l1)
    w1 = g_prob / (1.0 + e2)
    w2 = g_prob * e2 / (1.0 + e2)
    ex1, ex2 = i1 - N_GROUPS, i2 - N_GROUPS

    hot1 = lane == ex1
    hot2 = lane == ex2
    hot = jnp.where(hot1 | hot2, 1.0, 0.0)
    tri = (lax.broadcasted_iota(jnp.int32, (tm, tm), 0)
           > lax.broadcasted_iota(jnp.int32, (tm, tm), 1))
    prefix = jnp.dot(tri.astype(BF16), hot.astype(BF16), preferred_element_type=F32) + counts
    rank1 = jnp.sum(jnp.where(hot1, prefix, 0.0), axis=-1, keepdims=True)
    rank2 = jnp.sum(jnp.where(hot2, prefix, 0.0), axis=-1, keepdims=True)
    total = counts + jnp.sum(hot, axis=0, keepdims=True)

    meta = jnp.zeros((tm, LANES), F32)
    for idx, val in enumerate((ex1, ex2, w1, w2, rank1, rank2)):
        meta = jnp.where(lane == idx, val, meta)
    return meta, total


def _router_weights(w_router_group, w_router_expert):
    D = w_router_group.shape[0]
    w = jnp.concatenate(
        [w_router_group, jnp.transpose(w_router_expert, (1, 0, 2)).reshape(D, N_EXPERTS)], axis=1)
    return jnp.pad(w.astype(F32), ((0, 0), (0, LANES - N_GROUPS - N_EXPERTS)))


def _pack_bf16_pairs(x):
    m = x.shape[1] // 2
    lo = pltpu.bitcast(x[:, :m].astype(BF16).astype(F32), U32)
    hi = pltpu.bitcast(x[:, m:].astype(BF16).astype(F32), U32)
    return (lo >> 16) | hi


def _unpack_bf16_pairs(p):
    return pltpu.bitcast(p << 16, F32), pltpu.bitcast(p & jnp.uint32(0xFFFF0000), F32)


SLAB_ROWS = D_MODEL // 2 // LANES


def _matrix_to_slabs(mat):
    chunks = jnp.stack([mat[:, j * LANES:(j + 1) * LANES] for j in range(SLAB_ROWS)])
    return pltpu.einshape("stl->tsl", chunks)


def _slabs_to_matrix(slabs):
    chunks = pltpu.einshape("tsl->stl", slabs)
    return jnp.concatenate([chunks[j] for j in range(SLAB_ROWS)], axis=-1)


def _dispatch_kernel(pos_ref, pad_ref, h_ref, xs_ref, slabs, zeros, sem, zsem, *, tile, n_tok):
    i = pl.program_id(0)
    base = i * tile
    cur = i % 2

    def pad_copy(e):
        return pltpu.make_async_copy(zeros, xs_ref.at[pl.ds(pad_ref[e], MOE_TILE)], zsem)

    def tail_copy(t):
        return pltpu.make_async_copy(zeros, xs_ref.at[pl.ds(t * MOE_TILE, MOE_TILE)], zsem)

    @pl.when(i == 0)
    def _():
        zeros[...] = jnp.zeros_like(zeros)
        n_tiles = xs_ref.shape[0] // MOE_TILE
        for e in range(N_EXPERTS):
            @pl.when(pad_ref[e] >= 0)
            def _():
                pad_copy(e).start()
        lax.fori_loop(pad_ref[N_EXPERTS], n_tiles, lambda t, c: (tail_copy(t).start(), c)[1], 0)
        for e in range(N_EXPERTS):
            @pl.when(pad_ref[e] >= 0)
            def _():
                pad_copy(e).wait()
        lax.fori_loop(pad_ref[N_EXPERTS], n_tiles, lambda t, c: (tail_copy(t).wait(), c)[1], 0)

    def wait_rows(buf):
        for _ in range(2):
            pltpu.make_async_copy(slabs.at[buf], xs_ref.at[pl.ds(0, tile)], sem.at[buf]).wait()

    slabs[cur] = _matrix_to_slabs(_pack_bf16_pairs(h_ref[...]))

    def issue(r, c):
        for slot in range(2):
            pltpu.make_async_copy(slabs.at[cur, r], xs_ref.at[pos_ref[slot * n_tok + base + r]],
                                  sem.at[cur]).start()
        return c

    lax.fori_loop(0, tile, issue, 0, unroll=8)

    @pl.when(i > 0)
    def _():
        wait_rows(1 - cur)

    @pl.when(i == pl.num_programs(0) - 1)
    def _():
        wait_rows(cur)


def _dispatch(pos, pad_start, h, *, n_rows, tile=256):
    T, D = h.shape
    tile = min(tile, T)
    return pl.pallas_call(
        functools.partial(_dispatch_kernel, tile=tile, n_tok=T),
        grid_spec=pltpu.PrefetchScalarGridSpec(
            num_scalar_prefetch=2, grid=(T // tile,),
            in_specs=[pl.BlockSpec((tile, D), lambda i, pos, pad: (i, 0))],
            out_specs=pl.BlockSpec(memory_space=pl.ANY),
            scratch_shapes=[pltpu.VMEM((2, tile, SLAB_ROWS, LANES), U32),
                            pltpu.VMEM((MOE_TILE, SLAB_ROWS, LANES), U32),
                            pltpu.SemaphoreType.DMA((2,)), pltpu.SemaphoreType.DMA]),
        out_shape=jax.ShapeDtypeStruct((n_rows, SLAB_ROWS, LANES), U32),
        name="moe_dispatch", compiler_params=_cparams("arbitrary"))(pos, pad_start, h)


def _experts_kernel(te_ref, na_ref, xs_ref, wgu_ref, wd_ref, y_ref, wgu_bf, wd_bf):
    i = pl.program_id(0)

    @pl.when(i >= na_ref[0])
    def _():
        y_ref[...] = jnp.zeros_like(y_ref)

    @pl.when(i < na_ref[0])
    def _():
        @pl.when((i == 0) | (te_ref[i] != te_ref[jnp.maximum(i - 1, 0)]))
        def _():
            wgu_bf[...] = wgu_ref[0].astype(BF16)
            wd_bf[...] = wd_ref[0].astype(BF16)

        lo, hi = _unpack_bf16_pairs(_slabs_to_matrix(xs_ref[...]))
        x = jnp.concatenate([lo, hi], axis=-1).astype(BF16)
        gu = jnp.dot(x, wgu_bf[...], preferred_element_type=F32)
        hid = jax.nn.silu(gu[:, :EXPERT_FF]) * gu[:, EXPERT_FF:]
        y = jnp.dot(hid.astype(BF16), wd_bf[...], preferred_element_type=F32)
        y_ref[...] = _matrix_to_slabs(_pack_bf16_pairs(y))


def _experts(tile_expert, n_active, xs, w_gate_up, w_down, layer):
    n_rows = xs.shape[0]
    D = D_MODEL
    tm = MOE_TILE
    row_map = lambda i, te, na: (jnp.minimum(i, na[0] - 1), 0, 0)
    w_map = lambda i, te, na: (layer, te[jnp.minimum(i, na[0] - 1)], 0, 0)
    return pl.pallas_call(
        _experts_kernel,
        grid_spec=pltpu.PrefetchScalarGridSpec(
            num_scalar_prefetch=2, grid=(n_rows // tm,),
            in_specs=[pl.BlockSpec((tm, SLAB_ROWS, LANES), row_map),
                      pl.BlockSpec((None, 1, D, 2 * EXPERT_FF), w_map),
                      pl.BlockSpec((None, 1, EXPERT_FF, D), w_map)],
            out_specs=pl.BlockSpec((tm, SLAB_ROWS, LANES), lambda i, te, na: (i, 0, 0)),
            scratch_shapes=[pltpu.VMEM((D, 2 * EXPERT_FF), BF16), pltpu.VMEM((EXPERT_FF, D), BF16)]),
        out_shape=jax.ShapeDtypeStruct((n_rows, SLAB_ROWS, LANES), U32),
        name="moe_experts", compiler_params=_cparams("arbitrary"))(tile_expert, n_active, xs, w_gate_up, w_down)


def _combine_ln_kernel(pos_ref, h_ref, meta_ref, g_ref, b_ref, ys_ref, o_ref, ob_ref, slabs, sem,
                       *, tile, n_tok):
    i = pl.program_id(0)
    n_steps = pl.num_programs(0)
    cur = i % 2

    def gather_tile(step, buf):
        def issue(r, c):
            for slot in range(2):
                pltpu.make_async_copy(ys_ref.at[pos_ref[slot * n_tok + step * tile + r]],
                                      slabs.at[buf, slot, r], sem.at[buf]).start()
            return c

        lax.fori_loop(0, tile, issue, 0, unroll=8)

    @pl.when(i == 0)
    def _():
        gather_tile(0, 0)

    @pl.when(i + 1 < n_steps)
    def _():
        gather_tile(i + 1, 1 - cur)

    for slot in range(2):
        pltpu.make_async_copy(ys_ref.at[pl.ds(0, tile)], slabs.at[cur, slot], sem.at[cur]).wait()

    meta = meta_ref[...]
    halves = [None, None]
    for slot in range(2):
        lo, hi = _unpack_bf16_pairs(_slabs_to_matrix(slabs[cur, slot]))
        wgt = meta[:, 2 + slot:3 + slot]
        halves = [wgt * part if acc is None else acc + wgt * part
                  for acc, part in zip(halves, (lo, hi))]
    ffn = jnp.concatenate(halves, axis=-1)
    out = _layer_norm(ALPHA * h_ref[...] + ffn, g_ref[...], b_ref[...])
    o_ref[...] = out
    ob_ref[...] = out.astype(BF16)


def _combine_ln(pos, h, meta, gain, bias, ys, *, tile=256):
    T, D = h.shape
    tile = min(tile, T)
    tok = lambda i, pos: (i, 0)
    fixed = lambda i, pos: (0, 0)
    return pl.pallas_call(
        functools.partial(_combine_ln_kernel, tile=tile, n_tok=T),
        grid_spec=pltpu.PrefetchScalarGridSpec(
            num_scalar_prefetch=1, grid=(T // tile,),
            in_specs=[pl.BlockSpec((tile, D), tok), pl.BlockSpec((tile, LANES), tok),
                      pl.BlockSpec((1, D), fixed), pl.BlockSpec((1, D), fixed),
                      pl.BlockSpec(memory_space=pl.ANY)],
            out_specs=[pl.BlockSpec((tile, D), tok), pl.BlockSpec((tile, D), tok)],
            scratch_shapes=[pltpu.VMEM((2, 2, tile, SLAB_ROWS, LANES), U32), pltpu.SemaphoreType.DMA((2,))]),
        out_shape=[jax.ShapeDtypeStruct((T, D), F32), jax.ShapeDtypeStruct((T, D), BF16)],
        name="moe_combine_ln", compiler_params=_cparams("arbitrary"))(pos, h, meta, gain, bias, ys)


def _hierarchical_moe(h, meta, counts, w_gate_up, w_down, layer, gain, bias):
    T, D = h.shape

    cnt = counts[-1, 0, :N_EXPERTS].astype(jnp.int32)
    padded = (cnt + MOE_TILE - 1) // MOE_TILE * MOE_TILE
    ends = jnp.cumsum(padded)
    starts = ends - padded
    ex = meta[:, 0:2].astype(jnp.int32)
    rank = meta[:, 4:6].astype(jnp.int32)
    start_of = jnp.sum(jnp.where(ex[..., None] == jnp.arange(N_EXPERTS), starts, 0), axis=-1)
    pos = (start_of + rank).T.reshape(2 * T)
    n_rows = 2 * T + N_EXPERTS * MOE_TILE
    n_tiles = n_rows // MOE_TILE
    tile_start = jnp.arange(n_tiles, dtype=jnp.int32) * MOE_TILE
    tile_expert = jnp.minimum(jnp.sum(ends[None, :] <= tile_start[:, None], axis=1), N_EXPERTS - 1).astype(jnp.int32)
    n_active = (ends[-1:] // MOE_TILE).astype(jnp.int32)
    pad_start = jnp.concatenate([jnp.where(padded > 0, ends - MOE_TILE, -1).astype(jnp.int32), n_active])

    xs = _dispatch(pos, pad_start, h, n_rows=n_rows)
    ys = _experts(tile_expert, n_active, xs, w_gate_up, w_down, layer)
    return _combine_ln(pos, h, meta, gain, bias, ys)


def kernel(x, ln_gain, ln_bias, gdn_w_in, gdn_conv_w, gdn_a_log, gdn_dt_bias, gdn_norm_w, gdn_w_out, fox_w_in, fox_b_forget, fox_q_norm_w, fox_k_norm_w, fox_w_out, lru_w_in, lru_conv_w, lru_conv_b, lru_w_gate_a, lru_b_gate_a, lru_w_gate_x, lru_b_gate_x, lru_a_param, lru_w_out, moe_w_router_group, moe_w_router_expert, moe_w_gate_up, moe_w_down):
    B, S, D = x.shape
    h = x.reshape(B * S, D).astype(F32)
    hb = h
    row = lambda v: v.astype(F32).reshape(1, D)
    for layer in range(DEPTH):
        kind, j = layer % 3, layer // 3
        g0, b0 = row(ln_gain[layer, 0]), row(ln_bias[layer, 0])
        g1, b1 = row(ln_gain[layer, 1]), row(ln_bias[layer, 1])
        w_router = _router_weights(moe_w_router_group[layer], moe_w_router_expert[layer])
        if kind != 0:
            hb = hb.astype(BF16)
        if kind == 0:
            h, meta, counts = _gated_deltanet(
                hb, h, gdn_w_in, j, gdn_conv_w[j], gdn_a_log[j], gdn_dt_bias[j], gdn_norm_w[j],
                gdn_w_out[j], g0, b0, w_router, B=B, S=S)
        elif kind == 1:
            h, meta, counts = _forgetting_attention(
                hb, h, fox_w_in, j, fox_b_forget[j], fox_q_norm_w[j], fox_k_norm_w[j], fox_w_out[j],
                g0, b0, w_router, B=B, S=S)
        else:
            h, meta, counts = _rglru_block(
                hb, h, lru_w_in, j, lru_conv_w[j], lru_conv_b[j], lru_w_gate_a[j], lru_b_gate_a[j],
                lru_w_gate_x[j], lru_b_gate_x[j], lru_a_param[j], lru_w_out[j], g0, b0, w_router, B=B, S=S)
        h, hb = _hierarchical_moe(h, meta, counts, moe_w_gate_up, moe_w_down, layer, g1, b1)
    return h.reshape(B, S, D).astype(x.dtype)
```
